```python
import math
import jax, jax.numpy as jnp
from jax import lax
import numpy as np

D_MODEL = 1024
BATCH = 4
SEQ = 4096
DEPTH = 1

A_HEADS = 8
A_V_DIM = 64
A_WIDTH = A_HEADS * A_V_DIM
QK_NOPE_DIM = 64
QK_ROPE_DIM = 32
QK_DIM = QK_NOPE_DIM + QK_ROPE_DIM
Q_LORA_RANK = 256
KV_LORA_RANK = 128
ROPE_THETA = 10000.0
Q_BLOCK = 128
B_HEADS = 8
B_HEAD_DIM = 64
B_WIDTH = B_HEADS * B_HEAD_DIM
CHUNK = 128
D_MIX = A_WIDTH + B_WIDTH

IN_SPLITS = (Q_LORA_RANK, KV_LORA_RANK, QK_ROPE_DIM, A_WIDTH, B_WIDTH, B_WIDTH, B_WIDTH)
D_IN = Q_LORA_RANK + KV_LORA_RANK + QK_ROPE_DIM + A_WIDTH + 3 * B_WIDTH
EPS = 1e-6

kernel_name = "hybrid_mla_gmlp_parallel_groups"


def rms_norm(x, g):
    xf = x.astype(jnp.float32)
    y = xf * lax.rsqrt(jnp.mean(xf * xf, axis=-1, keepdims=True) + EPS)
    return (y * g.astype(jnp.float32)).astype(x.dtype)


def rope_cos_sin(positions):
    inv_freq = 1.0 / (ROPE_THETA ** (jnp.arange(0, QK_ROPE_DIM, 2, dtype=jnp.float32) / QK_ROPE_DIM))
    ang = positions.astype(jnp.float32)[..., None] * inv_freq
    return jnp.cos(ang)[:, :, None, :], jnp.sin(ang)[:, :, None, :]


def apply_rope(t, cos, sin):
    tf = t.astype(jnp.float32)
    t1, t2 = jnp.split(tf, 2, axis=-1)
    out = jnp.concatenate([t1 * cos - t2 * sin, t2 * cos + t1 * sin], axis=-1)
    return out.astype(t.dtype)


def split_cols(t, sizes):
    offs = np.cumsum(sizes)[:-1].tolist()
    return jnp.split(t, offs, axis=-1)


def setup_inputs(seed: int = 0) -> dict:
    key = jax.random.key(seed)
    ks = jax.random.split(key, 18)
    f32 = jnp.float32
    x = jax.random.normal(ks[0], (BATCH, SEQ, D_MODEL), f32)
    offset = jax.random.randint(ks[1], (BATCH, 1), 0, 1024, dtype=jnp.int32)
    positions = (jnp.arange(SEQ, dtype=jnp.int32)[None, :] + offset).astype(jnp.int32)
    gain = lambda k, shape: 1.0 + 0.02 * jax.random.normal(k, shape, f32)
    return {
        "x": x,
        "positions": positions,
        "norm_in_g": gain(ks[2], (D_MODEL,)),
        "w_in": jax.random.normal(ks[3], (D_MODEL, D_IN), f32) * D_MODEL ** -0.5,
        "q_lora_g": gain(ks[4], (Q_LORA_RANK,)),
        "w_uq": jax.random.normal(ks[5], (Q_LORA_RANK, A_HEADS * QK_DIM), f32) * Q_LORA_RANK ** -0.5,
        "kv_lora_g": gain(ks[6], (KV_LORA_RANK,)),
        "w_ukv": jax.random.normal(ks[7], (KV_LORA_RANK, A_HEADS * (QK_NOPE_DIM + A_V_DIM)), f32) * KV_LORA_RANK ** -0.5,
        "q_head_g": gain(ks[8], (QK_DIM,)),
        "k_head_g": gain(ks[9], (QK_DIM,)),
        "v_gate_g": gain(ks[10], (B_HEADS, B_HEAD_DIM)),
        "w_s": jax.random.normal(ks[11], (B_HEADS, CHUNK, CHUNK), f32) * CHUNK ** -0.5,
        "b_s": 0.02 * jax.random.normal(ks[12], (B_HEADS, CHUNK), f32),
        "out_a_g": gain(ks[13], (A_WIDTH,)),
        "out_b_g": gain(ks[14], (B_WIDTH,)),
        "w_out": jax.random.normal(ks[15], (D_MIX, D_MODEL), f32) * D_MIX ** -0.5,
    }


def mla_group(c_q, c_kv, k_rope, cos, sin, q_lora_g, w_uq, kv_lora_g, w_ukv, q_head_g, k_head_g):
    B, S, _ = c_q.shape
    q = (rms_norm(c_q, q_lora_g) @ w_uq).reshape(B, S, A_HEADS, QK_DIM)
    q_nope, q_pe = q[..., :QK_NOPE_DIM], q[..., QK_NOPE_DIM:]
    q = jnp.concatenate([q_nope, apply_rope(q_pe, cos, sin)], axis=-1)
    kv = (rms_norm(c_kv, kv_lora_g) @ w_ukv).reshape(B, S, A_HEADS, QK_NOPE_DIM + A_V_DIM)
    k_nope, v = kv[..., :QK_NOPE_DIM], kv[..., QK_NOPE_DIM:]
    k_pe = apply_rope(k_rope[:, :, None, :], cos, sin)
    k = jnp.concatenate([k_nope, jnp.broadcast_to(k_pe, (B, S, A_HEADS, QK_ROPE_DIM))], axis=-1)
    q = rms_norm(q, q_head_g)
    k = rms_norm(k, k_head_g)
    q = jnp.transpose(q, (0, 2, 1, 3))
    k = jnp.transpose(k, (0, 2, 1, 3))
    v = jnp.transpose(v, (0, 2, 1, 3))
    scale = 1.0 / math.sqrt(QK_DIM)
    n_blk = S // Q_BLOCK
    q_blocks = jnp.moveaxis(q.reshape(B, A_HEADS, n_blk, Q_BLOCK, QK_DIM), 2, 0)

    def attend(qb):
        s = jnp.einsum("bhqd,bhkd->bhqk", qb, k).astype(jnp.float32) * scale
        p = jax.nn.softmax(s, axis=-1).astype(v.dtype)
        return jnp.einsum("bhqk,bhkd->bhqd", p, v)

    o = lax.map(attend, q_blocks)
    o = jnp.transpose(o, (1, 0, 3, 2, 4)).reshape(B, S, A_WIDTH)
    return o


def gmlp_group(u, v, v_gate_g, w_s, b_s):
    B, S, _ = u.shape
    n_chunk = S // CHUNK
    u = jax.nn.gelu(u)
    v = rms_norm(jax.nn.gelu(v).reshape(B, S, B_HEADS, B_HEAD_DIM), v_gate_g)
    v = v.reshape(B, n_chunk, CHUNK, B_HEADS, B_HEAD_DIM)
    sv = jnp.einsum("hij,bcjhd->bcihd", w_s, v) + jnp.transpose(b_s)[None, None, :, :, None]
    out = u.reshape(B, n_chunk, CHUNK, B_HEADS, B_HEAD_DIM) * sv
    return out.reshape(B, S, B_WIDTH)


def reference(x, positions, norm_in_g, w_in, q_lora_g, w_uq, kv_lora_g, w_ukv, q_head_g, k_head_g,
              v_gate_g, w_s, b_s, out_a_g, out_b_g, w_out):
    cos, sin = rope_cos_sin(positions)
    for _ in range(DEPTH):
        h = rms_norm(x, norm_in_g)
        proj = h @ w_in
        c_q, c_kv, k_rope, z_a, u, v, z_b = split_cols(proj, IN_SPLITS)
        o_a = mla_group(c_q, c_kv, k_rope, cos, sin, q_lora_g, w_uq, kv_lora_g, w_ukv, q_head_g, k_head_g)
        o_b = gmlp_group(u, v, v_gate_g, w_s, b_s)
        o_a = rms_norm(o_a, out_a_g) * jax.nn.silu(z_a)
        o_b = rms_norm(o_b, out_b_g) * jax.nn.silu(z_b)
        x = x + jnp.concatenate([o_a, o_b], axis=-1) @ w_out
    return x
```

```python
import functools
import math

import jax
import jax.numpy as jnp
import numpy as np
from jax import lax
from jax.experimental import pallas as pl
from jax.experimental.pallas import tpu as pltpu

D_MODEL = 1024
A_HEADS = 8
A_V_DIM = 64
A_WIDTH = A_HEADS * A_V_DIM
QK_NOPE_DIM = 64
QK_ROPE_DIM = 32
HALF_ROPE = QK_ROPE_DIM // 2
QK_DIM = QK_NOPE_DIM + QK_ROPE_DIM
Q_LORA_RANK = 256
KV_LORA_RANK = 128
ROPE_THETA = 10000.0
B_HEADS = 8
B_HEAD_DIM = 64
B_WIDTH = B_HEADS * B_HEAD_DIM
CHUNK = 128
EPS = 1e-6

LANES = 128
HEAD_W = A_HEADS * LANES

OFF_CQ = 0
OFF_CKV = OFF_CQ + Q_LORA_RANK
OFF_KR = OFF_CKV + KV_LORA_RANK
OFF_ZA = OFF_KR + LANES
OFF_U = OFF_ZA + A_WIDTH
OFF_V = OFF_U + B_WIDTH
OFF_ZB = OFF_V + B_WIDTH
D_IN_EXT = OFF_ZB + B_WIDTH

TOK_TILE = 256
Q_TILE = 256
KV_TILE = 512
VMEM_LIMIT = 56 * 1024 * 1024

f32 = jnp.float32
bf16 = jnp.bfloat16


def _gelu(t):
    return jax.nn.gelu(t)


def _silu(t):
    return t * jax.nn.sigmoid(t)


def _proj_body(x_ref, pos_ref, g_in_ref, w_in_ref, gq_ref, w_uq_ref, gkv_ref, w_ukv_ref,
               qhg_ref, khg_ref, vgg_ref, ind_ref, wpair_ref, bpair_ref, gob_ref,
               invf_ref, sgn_ref, vone_ref,
               q_out, k_out, v_out, ga_out, mb_out):
    x = x_ref[...]
    ms = jnp.mean(x * x, axis=-1, keepdims=True)
    h = (x * lax.rsqrt(ms + EPS) * g_in_ref[...]).astype(bf16)

    def proj(lo, width):
        return jnp.dot(h, w_in_ref[:, lo:lo + width], preferred_element_type=f32)

    lane = lax.broadcasted_iota(jnp.int32, (1, LANES), 1)
    ang = pos_ref[...].astype(f32) * invf_ref[...]
    cos = jnp.cos(ang)
    sin = jnp.sin(ang)
    sin_signed = sin * sgn_ref[...]

    c_q = proj(OFF_CQ, Q_LORA_RANK)
    cqn = (c_q * lax.rsqrt(jnp.mean(c_q * c_q, axis=-1, keepdims=True) + EPS) * gq_ref[...]).astype(bf16)
    qq = jnp.dot(cqn, w_uq_ref[...], preferred_element_type=f32)
    qhg = qhg_ref[...]
    for hd in range(A_HEADS):
        qh = qq[:, hd * LANES:(hd + 1) * LANES] * cos + qq[:, HEAD_W + hd * LANES:HEAD_W + (hd + 1) * LANES] * sin_signed
        ss = jnp.sum(qh * qh, axis=-1, keepdims=True)
        qn = qh * lax.rsqrt(ss * (1.0 / QK_DIM) + EPS) * qhg
        q_out[0, hd] = qn.astype(bf16)

    c_kv = proj(OFF_CKV, KV_LORA_RANK)
    ckvn = (c_kv * lax.rsqrt(jnp.mean(c_kv * c_kv, axis=-1, keepdims=True) + EPS) * gkv_ref[...]).astype(bf16)
    kvv = jnp.dot(ckvn, w_ukv_ref[...], preferred_element_type=f32)
    kr = proj(OFF_KR, LANES)
    first = (lane >= QK_NOPE_DIM) & (lane < QK_NOPE_DIM + HALF_ROPE)
    second = (lane >= QK_NOPE_DIM + HALF_ROPE) & (lane < QK_DIM)
    kr_from_second = pltpu.roll(kr, LANES - HALF_ROPE, 1)
    kr_from_first = pltpu.roll(kr, HALF_ROPE, 1)
    kpe = kr * cos + jnp.where(first, kr_from_second, jnp.where(second, kr_from_first, 0.0)) * sin_signed
    kpe = jnp.where(first | second, kpe, 0.0)
    khg = khg_ref[...]
    for hd in range(A_HEADS):
        kh = kvv[:, hd * LANES:(hd + 1) * LANES] + kpe
        ss = jnp.sum(kh * kh, axis=-1, keepdims=True)
        kn = kh * lax.rsqrt(ss * (1.0 / QK_DIM) + EPS) * khg
        k_out[0, hd] = kn.astype(bf16)
        v_out[0, hd] = (kvv[:, HEAD_W + hd * LANES:HEAD_W + (hd + 1) * LANES]
                        + vone_ref[:, hd * LANES:(hd + 1) * LANES]).astype(bf16)

    ga_out[...] = _silu(proj(OFF_ZA, A_WIDTH)).astype(bf16)

    u = _gelu(proj(OFF_U, B_WIDTH))
    v = _gelu(proj(OFF_V, B_WIDTH))
    vv = v * v
    vv_hi = vv.astype(bf16)
    vv_lo = (vv - vv_hi.astype(f32)).astype(bf16)
    ind = ind_ref[...]
    ssq = (jnp.dot(vv_hi, ind, preferred_element_type=f32) + jnp.dot(vv_lo, ind, preferred_element_type=f32))
    vn = (v * lax.rsqrt(ssq * (1.0 / B_HEAD_DIM) + EPS) * vgg_ref[...]).astype(bf16)
    lower = lane < B_HEAD_DIM
    zero = jnp.zeros((), bf16)
    chunks = []
    for c in range(TOK_TILE // CHUNK):
        pairs = []
        for p in range(B_HEADS // 2):
            vp = vn[c * CHUNK:(c + 1) * CHUNK, p * LANES:(p + 1) * LANES]
            rhs = jnp.concatenate([jnp.where(lower, vp, zero), jnp.where(lower, zero, vp)], axis=0)
            sv = jnp.dot(wpair_ref[p], rhs, preferred_element_type=f32) + bpair_ref[p]
            pairs.append(u[c * CHUNK:(c + 1) * CHUNK, p * LANES:(p + 1) * LANES] * sv)
        chunks.append(jnp.concatenate(pairs, axis=1))
    ob = jnp.concatenate(chunks, axis=0)
    obn = ob * lax.rsqrt(jnp.mean(ob * ob, axis=-1, keepdims=True) + EPS) * gob_ref[...]
    mb_out[...] = (obn * _silu(proj(OFF_ZB, B_WIDTH))).astype(bf16)


def _attn_body(q_ref, k_ref, v_ref, ga_ref, mb_ref, x_ref, goa_ref, w_out_ref, y_ref, acc_scr):
    seq = k_ref.shape[2]

    def head_step(hd, _):
        qh = q_ref[0, hd]

        def kv_step(j, carry):
            m, acc = carry
            start = pl.multiple_of(j * KV_TILE, KV_TILE)
            ks = k_ref[0, hd, pl.ds(start, KV_TILE), :]
            vs = v_ref[0, hd, pl.ds(start, KV_TILE), :]
            s = lax.dot_general(qh, ks, (((1,), (1,)), ((), ())), preferred_element_type=f32)
            m_new = jnp.maximum(m, jnp.max(s, axis=-1, keepdims=True))
            alpha = jnp.exp(m - m_new)
            p = jnp.exp(s - m_new)
            acc = alpha * acc + jnp.dot(p.astype(bf16), vs, preferred_element_type=f32)
            return m_new, acc

        m0 = jnp.full((Q_TILE, 1), -jnp.inf, f32)
        acc0 = jnp.zeros((Q_TILE, LANES), f32)
        _, acc = lax.fori_loop(0, seq // KV_TILE, kv_step, (m0, acc0))
        acc_scr[hd] = acc
        return 0

    lax.fori_loop(0, A_HEADS, head_step, 0)

    lane = lax.broadcasted_iota(jnp.int32, (1, LANES), 1)
    lower = lane < A_V_DIM
    pairs = []
    for p in range(A_HEADS // 2):
        even = acc_scr[2 * p]
        odd = acc_scr[2 * p + 1]
        o_even = even / even[:, A_V_DIM:A_V_DIM + 1]
        o_odd = odd / odd[:, 0:1]
        pairs.append(jnp.where(lower, o_even, o_odd))
    o = jnp.concatenate(pairs, axis=1)
    on = o * lax.rsqrt(jnp.mean(o * o, axis=-1, keepdims=True) + EPS) * goa_ref[...]
    mix_a = (on * ga_ref[...].astype(f32)).astype(bf16)
    y = x_ref[...] + jnp.dot(mix_a, w_out_ref[0:A_WIDTH, :], preferred_element_type=f32)
    y = y + jnp.dot(mb_ref[...], w_out_ref[A_WIDTH:, :], preferred_element_type=f32)
    y_ref[...] = y


def _pad_heads(w, per_head, left=0):
    rows = w.shape[0]
    w = w.reshape(rows, A_HEADS, per_head)
    w = jnp.pad(w, ((0, 0), (0, 0), (left, LANES - per_head - left)))
    return w.reshape(rows, HEAD_W)


def _const_spec(shape):
    return pl.BlockSpec(shape, lambda *_: (0,) * len(shape))


def kernel(x, positions, norm_in_g, w_in, q_lora_g, w_uq, kv_lora_g, w_ukv, q_head_g, k_head_g,
           v_gate_g, w_s, b_s, out_a_g, out_b_g, w_out):
    batch, seq, _ = x.shape
    n_tok = batch * seq
    assert seq % Q_TILE == 0 and seq % KV_TILE == 0 and seq % TOK_TILE == 0 and TOK_TILE % CHUNK == 0

    splits = np.cumsum([Q_LORA_RANK, KV_LORA_RANK, QK_ROPE_DIM, A_WIDTH, B_WIDTH, B_WIDTH])
    wi_cq, wi_ckv, wi_kr, wi_za, wi_u, wi_v, wi_zb = jnp.split(w_in, splits.tolist(), axis=1)
    wi_kr = jnp.pad(wi_kr, ((0, 0), (QK_NOPE_DIM, LANES - QK_DIM)))
    w_in_ext = jnp.concatenate([wi_cq, wi_ckv, wi_kr, wi_za, wi_u, wi_v, wi_zb], axis=1).astype(bf16)

    uq = w_uq.reshape(Q_LORA_RANK, A_HEADS, QK_DIM)
    uq_swapped = jnp.concatenate(
        [jnp.zeros_like(uq[..., :QK_NOPE_DIM]), uq[..., QK_NOPE_DIM + HALF_ROPE:], uq[..., QK_NOPE_DIM:QK_NOPE_DIM + HALF_ROPE]],
        axis=-1)
    w_uq_ext = jnp.concatenate([_pad_heads(uq.reshape(Q_LORA_RANK, -1), QK_DIM),
                                _pad_heads(uq_swapped.reshape(Q_LORA_RANK, -1), QK_DIM)], axis=1).astype(bf16)

    ukv = w_ukv.reshape(KV_LORA_RANK, A_HEADS, QK_NOPE_DIM + A_V_DIM)
    uk = _pad_heads(ukv[..., :QK_NOPE_DIM].reshape(KV_LORA_RANK, -1), QK_NOPE_DIM)
    uv = ukv[..., QK_NOPE_DIM:]
    uv_even = jnp.pad(uv, ((0, 0), (0, 0), (0, LANES - A_V_DIM)))
    uv_odd = jnp.pad(uv, ((0, 0), (0, 0), (LANES - A_V_DIM, 0)))
    odd_head = (jnp.arange(A_HEADS) % 2 == 1)[None, :, None]
    uv_ext = jnp.where(odd_head, uv_odd, uv_even).reshape(KV_LORA_RANK, HEAD_W)
    w_ukv_ext = jnp.concatenate([uk, uv_ext], axis=1).astype(bf16)
    vone = np.zeros((1, A_HEADS, LANES), np.float32)
    vone[0, 0::2, A_V_DIM] = 1.0
    vone[0, 1::2, 0] = 1.0
    vone = jnp.asarray(vone.reshape(1, HEAD_W))

    scale = 1.0 / math.sqrt(QK_DIM)
    qhg = jnp.pad(q_head_g * scale, (0, LANES - QK_DIM)).reshape(1, LANES)
    khg = jnp.pad(k_head_g, (0, LANES - QK_DIM)).reshape(1, LANES)

    inv_freq = 1.0 / (ROPE_THETA ** (jnp.arange(0, QK_ROPE_DIM, 2, dtype=f32) / QK_ROPE_DIM))
    invf = jnp.concatenate([jnp.zeros((QK_NOPE_DIM,), f32), inv_freq, inv_freq,
                            jnp.zeros((LANES - QK_DIM,), f32)]).reshape(1, LANES)
    sgn = np.zeros((1, LANES), np.float32)
    sgn[0, QK_NOPE_DIM:QK_NOPE_DIM + HALF_ROPE] = -1.0
    sgn[0, QK_NOPE_DIM + HALF_ROPE:QK_DIM] = 1.0
    sgn = jnp.asarray(sgn)

    head_of = np.arange(B_WIDTH) // B_HEAD_DIM
    ind = jnp.asarray((head_of[:, None] == head_of[None, :]).astype(np.float32)).astype(bf16)
    wpair = jnp.concatenate([w_s[0::2], w_s[1::2]], axis=2).astype(bf16)
    bpair = jnp.concatenate([jnp.broadcast_to(b_s[0::2, :, None], (B_HEADS // 2, CHUNK, B_HEAD_DIM)),
                             jnp.broadcast_to(b_s[1::2, :, None], (B_HEADS // 2, CHUNK, B_HEAD_DIM))], axis=2)

    x2 = x.reshape(n_tok, D_MODEL)
    pos2 = positions.reshape(n_tok, 1)
    tiles_per_row = seq // TOK_TILE

    head_out = lambda t: pl.BlockSpec((1, A_HEADS, t, LANES), lambda i: (i // tiles_per_row, 0, i % tiles_per_row, 0))
    q, k, v, ga, mb = pl.pallas_call(
        _proj_body,
        grid=(n_tok // TOK_TILE,),
        in_specs=[
            pl.BlockSpec((TOK_TILE, D_MODEL), lambda i: (i, 0)),
            pl.BlockSpec((TOK_TILE, 1), lambda i: (i, 0)),
            _const_spec((1, D_MODEL)),
            _const_spec((D_MODEL, D_IN_EXT)),
            _const_spec((1, Q_LORA_RANK)),
            _const_spec((Q_LORA_RANK, 2 * HEAD_W)),
            _const_spec((1, KV_LORA_RANK)),
            _const_spec((KV_LORA_RANK, 2 * HEAD_W)),
            _const_spec((1, LANES)),
            _const_spec((1, LANES)),
            _const_spec((1, B_WIDTH)),
            _const_spec((B_WIDTH, B_WIDTH)),
            _const_spec((B_HEADS // 2, CHUNK, 2 * CHUNK)),
            _const_spec((B_HEADS // 2, CHUNK, LANES)),
            _const_spec((1, B_WIDTH)),
            _const_spec((1, LANES)),
            _const_spec((1, LANES)),
            _const_spec((1, HEAD_W)),
        ],
        out_specs=[
            head_out(TOK_TILE), head_out(TOK_TILE), head_out(TOK_TILE),
            pl.BlockSpec((TOK_TILE, A_WIDTH), lambda i: (i, 0)),
            pl.BlockSpec((TOK_TILE, B_WIDTH), lambda i: (i, 0)),
        ],
        out_shape=[
            jax.ShapeDtypeStruct((batch, A_HEADS, seq, LANES), bf16),
            jax.ShapeDtypeStruct((batch, A_HEADS, seq, LANES), bf16),
            jax.ShapeDtypeStruct((batch, A_HEADS, seq, LANES), bf16),
            jax.ShapeDtypeStruct((n_tok, A_WIDTH), bf16),
            jax.ShapeDtypeStruct((n_tok, B_WIDTH), bf16),
        ],
        compiler_params=pltpu.CompilerParams(dimension_semantics=("arbitrary",), vmem_limit_bytes=VMEM_LIMIT),
        name="proj",
    )(x2, pos2, norm_in_g.reshape(1, D_MODEL), w_in_ext, q_lora_g.reshape(1, -1), w_uq_ext,
      kv_lora_g.reshape(1, -1), w_ukv_ext, qhg, khg, v_gate_g.reshape(1, B_WIDTH), ind, wpair, bpair,
      out_b_g.reshape(1, B_WIDTH), invf, sgn, vone)

    q_tiles = seq // Q_TILE
    tok_blk = lambda w: pl.BlockSpec((Q_TILE, w), lambda b, i: (b * q_tiles + i, 0))
    y = pl.pallas_call(
        _attn_body,
        grid=(batch, q_tiles),
        in_specs=[
            pl.BlockSpec((1, A_HEADS, Q_TILE, LANES), lambda b, i: (b, 0, i, 0)),
            pl.BlockSpec((1, A_HEADS, seq, LANES), lambda b, i: (b, 0, 0, 0)),
            pl.BlockSpec((1, A_HEADS, seq, LANES), lambda b, i: (b, 0, 0, 0)),
            tok_blk(A_WIDTH), tok_blk(B_WIDTH), tok_blk(D_MODEL),
            pl.BlockSpec((1, A_WIDTH), lambda b, i: (0, 0)),
            pl.BlockSpec((D_MODEL, D_MODEL), lambda b, i: (0, 0)),
        ],
        out_specs=tok_blk(D_MODEL),
        out_shape=jax.ShapeDtypeStruct((n_tok, D_MODEL), f32),
        scratch_shapes=[pltpu.VMEM((A_HEADS, Q_TILE, LANES), f32)],
        compiler_params=pltpu.CompilerParams(dimension_semantics=("arbitrary", "arbitrary"),
                                             vmem_limit_bytes=VMEM_LIMIT),
        name="attn",
    )(q, k, v, ga, mb, x2, out_a_g.reshape(1, A_WIDTH), w_out.astype(bf16))
    return y.reshape(batch, seq, D_MODEL)
```

```python
import math

import jax
import jax.numpy as jnp
import numpy as np
from jax import lax
from jax.experimental import pallas as pl
from jax.experimental.pallas import tpu as pltpu

D_MODEL = 1024
A_HEADS = 8
A_V_DIM = 64
A_WIDTH = A_HEADS * A_V_DIM
QK_NOPE_DIM = 64
QK_ROPE_DIM = 32
HALF_ROPE = QK_ROPE_DIM // 2
QK_DIM = QK_NOPE_DIM + QK_ROPE_DIM
Q_LORA_RANK = 256
KV_LORA_RANK = 128
ROPE_THETA = 10000.0
B_HEADS = 8
B_HEAD_DIM = 64
B_WIDTH = B_HEADS * B_HEAD_DIM
CHUNK = 128
EPS = 1e-6

LANES = 128
HEAD_W = A_HEADS * LANES

OFF_CQ = 0
OFF_CKV = OFF_CQ + Q_LORA_RANK
OFF_KR = OFF_CKV + KV_LORA_RANK
OFF_ZA = OFF_KR + LANES
OFF_U = OFF_ZA + A_WIDTH
OFF_V = OFF_U + B_WIDTH
OFF_ZB = OFF_V + B_WIDTH
D_IN_EXT = OFF_ZB + B_WIDTH

TOK_TILE = 256
Q_TILE = 256
KV_TILE = 512
VMEM_LIMIT = 56 * 1024 * 1024

f32 = jnp.float32
bf16 = jnp.bfloat16


def _gelu(t):
    return jax.nn.gelu(t)


def _silu(t):
    return t * jax.nn.sigmoid(t)


def _proj_body(x_ref, pos_ref, g_in_ref, w_in_ref, gq_ref, w_uq_ref, gkv_ref, w_ukv_ref,
               qhg_ref, khg_ref, vgg_ref, ind_ref, wpair_ref, bpair_ref, gob_ref,
               invf_ref, sgn_ref, vone_ref,
               q_out, k_out, v_out, ga_out, mb_out):
    x = x_ref[...]
    ms = jnp.mean(x * x, axis=-1, keepdims=True)
    h = (x * lax.rsqrt(ms + EPS) * g_in_ref[...]).astype(bf16)

    def proj(lo, width):
        return jnp.dot(h, w_in_ref[:, lo:lo + width], preferred_element_type=f32)

    lane = lax.broadcasted_iota(jnp.int32, (1, LANES), 1)
    ang = pos_ref[...].astype(f32) * invf_ref[...]
    cos = jnp.cos(ang)
    sin = jnp.sin(ang)
    sin_signed = sin * sgn_ref[...]

    c_q = proj(OFF_CQ, Q_LORA_RANK)
    cqn = (c_q * lax.rsqrt(jnp.mean(c_q * c_q, axis=-1, keepdims=True) + EPS) * gq_ref[...]).astype(bf16)
    qq = jnp.dot(cqn, w_uq_ref[...], preferred_element_type=f32)
    qhg = qhg_ref[...]
    for hd in range(A_HEADS):
        qh = qq[:, hd * LANES:(hd + 1) * LANES] * cos + qq[:, HEAD_W + hd * LANES:HEAD_W + (hd + 1) * LANES] * sin_signed
        ss = jnp.sum(qh * qh, axis=-1, keepdims=True)
        qn = qh * lax.rsqrt(ss * (1.0 / QK_DIM) + EPS) * qhg
        q_out[0, hd] = qn.astype(bf16)

    c_kv = proj(OFF_CKV, KV_LORA_RANK)
    ckvn = (c_kv * lax.rsqrt(jnp.mean(c_kv * c_kv, axis=-1, keepdims=True) + EPS) * gkv_ref[...]).astype(bf16)
    kvv = jnp.dot(ckvn, w_ukv_ref[...], preferred_element_type=f32)
    kr = proj(OFF_KR, LANES)
    first = (lane >= QK_NOPE_DIM) & (lane < QK_NOPE_DIM + HALF_ROPE)
    second = (lane >= QK_NOPE_DIM + HALF_ROPE) & (lane < QK_DIM)
    kr_from_second = pltpu.roll(kr, LANES - HALF_ROPE, 1)
    kr_from_first = pltpu.roll(kr, HALF_ROPE, 1)
    kpe = kr * cos + jnp.where(first, kr_from_second, jnp.where(second, kr_from_first, 0.0)) * sin_signed
    kpe = jnp.where(first | second, kpe, 0.0)
    khg = khg_ref[...]
    for hd in range(A_HEADS):
        kh = kvv[:, hd * LANES:(hd + 1) * LANES] + kpe
        ss = jnp.sum(kh * kh, axis=-1, keepdims=True)
        kn = kh * lax.rsqrt(ss * (1.0 / QK_DIM) + EPS) * khg
        k_out[0, hd] = kn.astype(bf16)
        v_out[0, hd] = (kvv[:, HEAD_W + hd * LANES:HEAD_W + (hd + 1) * LANES]
                        + vone_ref[:, hd * LANES:(hd + 1) * LANES]).astype(bf16)

    ga_out[...] = _silu(proj(OFF_ZA, A_WIDTH)).astype(bf16)

    u = _gelu(proj(OFF_U, B_WIDTH))
    v = _gelu(proj(OFF_V, B_WIDTH))
    vv = v * v
    vv_hi = vv.astype(bf16)
    vv_lo = (vv - vv_hi.astype(f32)).astype(bf16)
    ind = ind_ref[...]
    ssq = (jnp.dot(vv_hi, ind, preferred_element_type=f32) + jnp.dot(vv_lo, ind, preferred_element_type=f32))
    vn = (v * lax.rsqrt(ssq * (1.0 / B_HEAD_DIM) + EPS) * vgg_ref[...]).astype(bf16)
    lower = lane < B_HEAD_DIM
    zero = jnp.zeros((), bf16)
    chunks = []
    for c in range(TOK_TILE // CHUNK):
        pairs = []
        for p in range(B_HEADS // 2):
            vp = vn[c * CHUNK:(c + 1) * CHUNK, p * LANES:(p + 1) * LANES]
            rhs = jnp.concatenate([jnp.where(lower, vp, zero), jnp.where(lower, zero, vp)], axis=0)
            sv = jnp.dot(wpair_ref[p], rhs, preferred_element_type=f32) + bpair_ref[p]
            pairs.append(u[c * CHUNK:(c + 1) * CHUNK, p * LANES:(p + 1) * LANES] * sv)
        chunks.append(jnp.concatenate(pairs, axis=1))
    ob = jnp.concatenate(chunks, axis=0)
    obn = ob * lax.rsqrt(jnp.mean(ob * ob, axis=-1, keepdims=True) + EPS) * gob_ref[...]
    mb_out[...] = (obn * _silu(proj(OFF_ZB, B_WIDTH))).astype(bf16)


def _attn_body(q_ref, k_ref, v_ref, ga_ref, mb_ref, x_ref, goa_ref, w_out_ref, y_ref,
               s_even, s_odd, acc_scr):
    seq = k_ref.shape[2]
    n_kv = seq // KV_TILE

    def head_step(hd, parity, m_prev, do_scores=True, do_weighted=True):
        s_cur, s_prev = (s_even, s_odd) if parity == 0 else (s_odd, s_even)
        qh = q_ref[0, hd] if do_scores else None
        m = acc = None
        for t in range(n_kv):
            cols = slice(t * KV_TILE, (t + 1) * KV_TILE)
            if do_scores:
                s = lax.dot_general(qh, k_ref[0, hd, cols, :], (((1,), (1,)), ((), ())),
                                    preferred_element_type=f32)
                s_cur[:, cols] = s
                m = s if m is None else jnp.maximum(m, s)
            if do_weighted:
                p = jnp.exp2(s_prev[:, cols] - m_prev).astype(bf16)
                pv = jnp.dot(p, v_ref[0, hd - 1, cols, :], preferred_element_type=f32)
                acc = pv if acc is None else acc + pv
        if do_weighted:
            acc_scr[hd - 1] = acc
        return jnp.max(m, axis=-1, keepdims=True) if do_scores else None

    def pair_step(i, m_prev):
        m_odd = head_step(2 * i + 1, 1, m_prev)
        return head_step(2 * i + 2, 0, m_odd)

    m_cur = head_step(0, 0, None, do_weighted=False)
    m_cur = lax.fori_loop(0, (A_HEADS - 2) // 2, pair_step, m_cur)
    m_cur = head_step(A_HEADS - 1, 1, m_cur)
    head_step(A_HEADS, 0, m_cur, do_scores=False)

    lane = lax.broadcasted_iota(jnp.int32, (1, LANES), 1)
    lower = lane < A_V_DIM
    pairs = []
    for p in range(A_HEADS // 2):
        even = acc_scr[2 * p]
        odd = acc_scr[2 * p + 1]
        o_even = even / even[:, A_V_DIM:A_V_DIM + 1]
        o_odd = odd / odd[:, 0:1]
        pairs.append(jnp.where(lower, o_even, o_odd))
    o = jnp.concatenate(pairs, axis=1)
    on = o * lax.rsqrt(jnp.mean(o * o, axis=-1, keepdims=True) + EPS) * goa_ref[...]
    mix_a = (on * ga_ref[...].astype(f32)).astype(bf16)
    y = x_ref[...] + jnp.dot(mix_a, w_out_ref[0:A_WIDTH, :], preferred_element_type=f32)
    y = y + jnp.dot(mb_ref[...], w_out_ref[A_WIDTH:, :], preferred_element_type=f32)
    y_ref[...] = y


def _pad_heads(w, per_head, left=0):
    rows = w.shape[0]
    w = w.reshape(rows, A_HEADS, per_head)
    w = jnp.pad(w, ((0, 0), (0, 0), (left, LANES - per_head - left)))
    return w.reshape(rows, HEAD_W)


def _const_spec(shape):
    return pl.BlockSpec(shape, lambda *_: (0,) * len(shape))


def kernel(x, positions, norm_in_g, w_in, q_lora_g, w_uq, kv_lora_g, w_ukv, q_head_g, k_head_g,
           v_gate_g, w_s, b_s, out_a_g, out_b_g, w_out):
    batch, seq, _ = x.shape
    n_tok = batch * seq
    assert seq % Q_TILE == 0 and seq % KV_TILE == 0 and seq % TOK_TILE == 0 and TOK_TILE % CHUNK == 0

    splits = np.cumsum([Q_LORA_RANK, KV_LORA_RANK, QK_ROPE_DIM, A_WIDTH, B_WIDTH, B_WIDTH])
    wi_cq, wi_ckv, wi_kr, wi_za, wi_u, wi_v, wi_zb = jnp.split(w_in, splits.tolist(), axis=1)
    wi_kr = jnp.pad(wi_kr, ((0, 0), (QK_NOPE_DIM, LANES - QK_DIM)))
    w_in_ext = jnp.concatenate([wi_cq, wi_ckv, wi_kr, wi_za, wi_u, wi_v, wi_zb], axis=1).astype(bf16)

    uq = w_uq.reshape(Q_LORA_RANK, A_HEADS, QK_DIM)
    uq_swapped = jnp.concatenate(
        [jnp.zeros_like(uq[..., :QK_NOPE_DIM]), uq[..., QK_NOPE_DIM + HALF_ROPE:], uq[..., QK_NOPE_DIM:QK_NOPE_DIM + HALF_ROPE]],
        axis=-1)
    w_uq_ext = jnp.concatenate([_pad_heads(uq.reshape(Q_LORA_RANK, -1), QK_DIM),
                                _pad_heads(uq_swapped.reshape(Q_LORA_RANK, -1), QK_DIM)], axis=1).astype(bf16)

    ukv = w_ukv.reshape(KV_LORA_RANK, A_HEADS, QK_NOPE_DIM + A_V_DIM)
    uk = _pad_heads(ukv[..., :QK_NOPE_DIM].reshape(KV_LORA_RANK, -1), QK_NOPE_DIM)
    uv = ukv[..., QK_NOPE_DIM:]
    uv_even = jnp.pad(uv, ((0, 0), (0, 0), (0, LANES - A_V_DIM)))
    uv_odd = jnp.pad(uv, ((0, 0), (0, 0), (LANES - A_V_DIM, 0)))
    odd_head = (jnp.arange(A_HEADS) % 2 == 1)[None, :, None]
    uv_ext = jnp.where(odd_head, uv_odd, uv_even).reshape(KV_LORA_RANK, HEAD_W)
    w_ukv_ext = jnp.concatenate([uk, uv_ext], axis=1).astype(bf16)
    vone = np.zeros((1, A_HEADS, LANES), np.float32)
    vone[0, 0::2, A_V_DIM] = 1.0
    vone[0, 1::2, 0] = 1.0
    vone = jnp.asarray(vone.reshape(1, HEAD_W))

    scale = math.log2(math.e) / math.sqrt(QK_DIM)
    qhg = jnp.pad(q_head_g * scale, (0, LANES - QK_DIM)).reshape(1, LANES)
    khg = jnp.pad(k_head_g, (0, LANES - QK_DIM)).reshape(1, LANES)

    inv_freq = 1.0 / (ROPE_THETA ** (jnp.arange(0, QK_ROPE_DIM, 2, dtype=f32) / QK_ROPE_DIM))
    invf = jnp.concatenate([jnp.zeros((QK_NOPE_DIM,), f32), inv_freq, inv_freq,
                            jnp.zeros((LANES - QK_DIM,), f32)]).reshape(1, LANES)
    sgn = np.zeros((1, LANES), np.float32)
    sgn[0, QK_NOPE_DIM:QK_NOPE_DIM + HALF_ROPE] = -1.0
    sgn[0, QK_NOPE_DIM + HALF_ROPE:QK_DIM] = 1.0
    sgn = jnp.asarray(sgn)

    head_of = np.arange(B_WIDTH) // B_HEAD_DIM
    ind = jnp.asarray((head_of[:, None] == head_of[None, :]).astype(np.float32)).astype(bf16)
    wpair = jnp.concatenate([w_s[0::2], w_s[1::2]], axis=2).astype(bf16)
    bpair = jnp.concatenate([jnp.broadcast_to(b_s[0::2, :, None], (B_HEADS // 2, CHUNK, B_HEAD_DIM)),
                             jnp.broadcast_to(b_s[1::2, :, None], (B_HEADS // 2, CHUNK, B_HEAD_DIM))], axis=2)

    x2 = x.reshape(n_tok, D_MODEL)
    pos2 = positions.reshape(n_tok, 1)
    tiles_per_row = seq // TOK_TILE

    head_out = lambda t: pl.BlockSpec((1, A_HEADS, t, LANES), lambda i: (i // tiles_per_row, 0, i % tiles_per_row, 0))
    q, k, v, ga, mb = pl.pallas_call(
        _proj_body,
        grid=(n_tok // TOK_TILE,),
        in_specs=[
            pl.BlockSpec((TOK_TILE, D_MODEL), lambda i: (i, 0)),
            pl.BlockSpec((TOK_TILE, 1), lambda i: (i, 0)),
            _const_spec((1, D_MODEL)),
            _const_spec((D_MODEL, D_IN_EXT)),
            _const_spec((1, Q_LORA_RANK)),
            _const_spec((Q_LORA_RANK, 2 * HEAD_W)),
            _const_spec((1, KV_LORA_RANK)),
            _const_spec((KV_LORA_RANK, 2 * HEAD_W)),
            _const_spec((1, LANES)),
            _const_spec((1, LANES)),
            _const_spec((1, B_WIDTH)),
            _const_spec((B_WIDTH, B_WIDTH)),
            _const_spec((B_HEADS // 2, CHUNK, 2 * CHUNK)),
            _const_spec((B_HEADS // 2, CHUNK, LANES)),
            _const_spec((1, B_WIDTH)),
            _const_spec((1, LANES)),
            _const_spec((1, LANES)),
            _const_spec((1, HEAD_W)),
        ],
        out_specs=[
            head_out(TOK_TILE), head_out(TOK_TILE), head_out(TOK_TILE),
            pl.BlockSpec((TOK_TILE, A_WIDTH), lambda i: (i, 0)),
            pl.BlockSpec((TOK_TILE, B_WIDTH), lambda i: (i, 0)),
        ],
        out_shape=[
            jax.ShapeDtypeStruct((batch, A_HEADS, seq, LANES), bf16),
            jax.ShapeDtypeStruct((batch, A_HEADS, seq, LANES), bf16),
            jax.ShapeDtypeStruct((batch, A_HEADS, seq, LANES), bf16),
            jax.ShapeDtypeStruct((n_tok, A_WIDTH), bf16),
            jax.ShapeDtypeStruct((n_tok, B_WIDTH), bf16),
        ],
        compiler_params=pltpu.CompilerParams(dimension_semantics=("arbitrary",), vmem_limit_bytes=VMEM_LIMIT),
        name="proj",
    )(x2, pos2, norm_in_g.reshape(1, D_MODEL), w_in_ext, q_lora_g.reshape(1, -1), w_uq_ext,
      kv_lora_g.reshape(1, -1), w_ukv_ext, qhg, khg, v_gate_g.reshape(1, B_WIDTH), ind, wpair, bpair,
      out_b_g.reshape(1, B_WIDTH), invf, sgn, vone)

    q_tiles = seq // Q_TILE
    tok_blk = lambda w: pl.BlockSpec((Q_TILE, w), lambda b, i: (b * q_tiles + i, 0))
    y = pl.pallas_call(
        _attn_body,
        grid=(batch, q_tiles),
        in_specs=[
            pl.BlockSpec((1, A_HEADS, Q_TILE, LANES), lambda b, i: (b, 0, i, 0)),
            pl.BlockSpec((1, A_HEADS, seq, LANES), lambda b, i: (b, 0, 0, 0)),
            pl.BlockSpec((1, A_HEADS, seq, LANES), lambda b, i: (b, 0, 0, 0)),
            tok_blk(A_WIDTH), tok_blk(B_WIDTH), tok_blk(D_MODEL),
            pl.BlockSpec((1, A_WIDTH), lambda b, i: (0, 0)),
            pl.BlockSpec((D_MODEL, D_MODEL), lambda b, i: (0, 0)),
        ],
        out_specs=tok_blk(D_MODEL),
        out_shape=jax.ShapeDtypeStruct((n_tok, D_MODEL), f32),
        scratch_shapes=[pltpu.VMEM((Q_TILE, seq), f32), pltpu.VMEM((Q_TILE, seq), f32),
                        pltpu.VMEM((A_HEADS, Q_TILE, LANES), f32)],
        compiler_params=pltpu.CompilerParams(dimension_semantics=("arbitrary", "arbitrary"),
                                             vmem_limit_bytes=VMEM_LIMIT),
        name="attn",
    )(q, k, v, ga, mb, x2, out_a_g.reshape(1, A_WIDTH), w_out.astype(bf16))
    return y.reshape(batch, seq, D_MODEL)
```

```python
import math

import jax
import jax.numpy as jnp
import numpy as np
from jax import lax
from jax.experimental import pallas as pl
from jax.experimental.pallas import tpu as pltpu

D_MODEL = 1024
A_HEADS = 8
A_V_DIM = 64
A_WIDTH = A_HEADS * A_V_DIM
QK_NOPE_DIM = 64
QK_ROPE_DIM = 32
HALF_ROPE = QK_ROPE_DIM // 2
QK_DIM = QK_NOPE_DIM + QK_ROPE_DIM
Q_LORA_RANK = 256
KV_LORA_RANK = 128
ROPE_THETA = 10000.0
B_HEADS = 8
B_HEAD_DIM = 64
B_WIDTH = B_HEADS * B_HEAD_DIM
CHUNK = 128
EPS = 1e-6

LANES = 128
HEAD_W = A_HEADS * LANES
V_ROWS = 80

OFF_CQ = 0
OFF_CKV = OFF_CQ + Q_LORA_RANK
OFF_KR = OFF_CKV + KV_LORA_RANK
OFF_ZA = OFF_KR + LANES
OFF_U = OFF_ZA + A_WIDTH
OFF_V = OFF_U + B_WIDTH
OFF_ZB = OFF_V + B_WIDTH
D_IN_EXT = OFF_ZB + B_WIDTH

TOK_TILE = 256
Q_TILE = 256
KV_TILE = 512
VMEM_LIMIT = 56 * 1024 * 1024

f32 = jnp.float32
bf16 = jnp.bfloat16


def _gelu(t):
    return jax.nn.gelu(t)


def _silu(t):
    return t * jax.nn.sigmoid(t)


def _proj_body(x_ref, pos_ref, g_in_ref, w_in_ref, gq_ref, w_uq_ref, gkv_ref, w_ukv_ref,
               qhg_ref, khg_ref, vgg_ref, ind_ref, wpair_ref, bpair_ref, gob_ref,
               invf_ref, sgn_ref, vone_ref,
               q_out, k_out, v_out, ga_out, mb_out):
    x = x_ref[...]
    ms = jnp.mean(x * x, axis=-1, keepdims=True)
    h = (x * lax.rsqrt(ms + EPS) * g_in_ref[...]).astype(bf16)

    def proj(lo, width):
        return jnp.dot(h, w_in_ref[:, lo:lo + width], preferred_element_type=f32)

    lane = lax.broadcasted_iota(jnp.int32, (1, LANES), 1)
    ang = pos_ref[...].astype(f32) * invf_ref[...]
    cos = jnp.cos(ang)
    sin = jnp.sin(ang)
    sin_signed = sin * sgn_ref[...]

    c_q = proj(OFF_CQ, Q_LORA_RANK)
    cqn = (c_q * lax.rsqrt(jnp.mean(c_q * c_q, axis=-1, keepdims=True) + EPS) * gq_ref[...]).astype(bf16)
    qq = jnp.dot(cqn, w_uq_ref[...], preferred_element_type=f32)
    qhg = qhg_ref[...]
    for hd in range(A_HEADS):
        qh = qq[:, hd * LANES:(hd + 1) * LANES] * cos + qq[:, HEAD_W + hd * LANES:HEAD_W + (hd + 1) * LANES] * sin_signed
        ss = jnp.sum(qh * qh, axis=-1, keepdims=True)
        qn = qh * lax.rsqrt(ss * (1.0 / QK_DIM) + EPS) * qhg
        q_out[0, hd] = qn.astype(bf16)

    c_kv = proj(OFF_CKV, KV_LORA_RANK)
    ckvn = (c_kv * lax.rsqrt(jnp.mean(c_kv * c_kv, axis=-1, keepdims=True) + EPS) * gkv_ref[...]).astype(bf16)
    kvv = jnp.dot(ckvn, w_ukv_ref[...], preferred_element_type=f32)
    kr = proj(OFF_KR, LANES)
    first = (lane >= QK_NOPE_DIM) & (lane < QK_NOPE_DIM + HALF_ROPE)
    second = (lane >= QK_NOPE_DIM + HALF_ROPE) & (lane < QK_DIM)
    kr_from_second = pltpu.roll(kr, LANES - HALF_ROPE, 1)
    kr_from_first = pltpu.roll(kr, HALF_ROPE, 1)
    kpe = kr * cos + jnp.where(first, kr_from_second, jnp.where(second, kr_from_first, 0.0)) * sin_signed
    kpe = jnp.where(first | second, kpe, 0.0)
    khg = khg_ref[...]
    for hd in range(A_HEADS):
        kh = kvv[:, hd * LANES:(hd + 1) * LANES] + kpe
        ss = jnp.sum(kh * kh, axis=-1, keepdims=True)
        kn = kh * lax.rsqrt(ss * (1.0 / QK_DIM) + EPS) * khg
        k_out[0, hd] = kn.astype(bf16)
        vh = kvv[:, HEAD_W + hd * LANES:HEAD_W + (hd + 1) * LANES] + vone_ref[...]
        v_out[0, hd] = vh.T[0:V_ROWS].astype(bf16)

    ga_out[...] = _silu(proj(OFF_ZA, A_WIDTH)).astype(bf16)

    u = _gelu(proj(OFF_U, B_WIDTH))
    v = _gelu(proj(OFF_V, B_WIDTH))
    vv = v * v
    vv_hi = vv.astype(bf16)
    vv_lo = (vv - vv_hi.astype(f32)).astype(bf16)
    ind = ind_ref[...]
    ssq = (jnp.dot(vv_hi, ind, preferred_element_type=f32) + jnp.dot(vv_lo, ind, preferred_element_type=f32))
    vn = (v * lax.rsqrt(ssq * (1.0 / B_HEAD_DIM) + EPS) * vgg_ref[...]).astype(bf16)
    lower = lane < B_HEAD_DIM
    zero = jnp.zeros((), bf16)
    chunks = []
    for c in range(TOK_TILE // CHUNK):
        pairs = []
        for p in range(B_HEADS // 2):
            vp = vn[c * CHUNK:(c + 1) * CHUNK, p * LANES:(p + 1) * LANES]
            rhs = jnp.concatenate([jnp.where(lower, vp, zero), jnp.where(lower, zero, vp)], axis=0)
            sv = jnp.dot(wpair_ref[p], rhs, preferred_element_type=f32) + bpair_ref[p]
            pairs.append(u[c * CHUNK:(c + 1) * CHUNK, p * LANES:(p + 1) * LANES] * sv)
        chunks.append(jnp.concatenate(pairs, axis=1))
    ob = jnp.concatenate(chunks, axis=0)
    obn = ob * lax.rsqrt(jnp.mean(ob * ob, axis=-1, keepdims=True) + EPS) * gob_ref[...]
    mb_out[...] = (obn * _silu(proj(OFF_ZB, B_WIDTH))).astype(bf16)


def _attn_body(q_ref, k_ref, v_ref, ga_ref, mb_ref, x_ref, goa_ref, w_out_ref, y_ref,
               s_even, s_odd, acc_scr):
    seq = k_ref.shape[2]
    n_kv = seq // KV_TILE

    def head_step(hd, parity, m_prev, do_scores=True, do_weighted=True):
        s_cur, s_prev = (s_even, s_odd) if parity == 0 else (s_odd, s_even)
        q_t = q_ref[0, hd].T if do_scores else None
        m = acc = None
        for t in range(n_kv):
            keys = slice(t * KV_TILE, (t + 1) * KV_TILE)
            if do_scores:
                s = jnp.dot(k_ref[0, hd, keys, :], q_t, preferred_element_type=f32)
                s_cur[keys, :] = s
                m = s if m is None else jnp.maximum(m, s)
            if do_weighted:
                p = jnp.exp2(s_prev[keys, :] - m_prev).astype(bf16)
                pv = jnp.dot(v_ref[0, hd - 1, :, keys], p, preferred_element_type=f32)
                acc = pv if acc is None else acc + pv
        if do_weighted:
            acc_scr[hd - 1] = acc
        return jnp.max(m, axis=0, keepdims=True) if do_scores else None

    def pair_step(i, m_prev):
        m_odd = head_step(2 * i + 1, 1, m_prev)
        return head_step(2 * i + 2, 0, m_odd)

    m_cur = head_step(0, 0, None, do_weighted=False)
    m_cur = lax.fori_loop(0, (A_HEADS - 2) // 2, pair_step, m_cur)
    m_cur = head_step(A_HEADS - 1, 1, m_cur)
    half = D_MODEL // 2
    y_b = [x_ref[:, c:c + half] + jnp.dot(mb_ref[...], w_out_ref[A_WIDTH:, c:c + half], preferred_element_type=f32)
           for c in (0, half)]
    head_step(A_HEADS, 0, m_cur, do_scores=False)

    heads = []
    for hd in range(A_HEADS):
        acc = acc_scr[hd]
        heads.append(acc[0:A_V_DIM] / acc[A_V_DIM:A_V_DIM + 1])
    o = jnp.concatenate(heads, axis=0).T
    on = o * lax.rsqrt(jnp.mean(o * o, axis=-1, keepdims=True) + EPS) * goa_ref[...]
    mix_a = (on * ga_ref[...].astype(f32)).astype(bf16)
    for i, c in enumerate((0, half)):
        y_ref[:, c:c + half] = y_b[i] + jnp.dot(mix_a, w_out_ref[0:A_WIDTH, c:c + half], preferred_element_type=f32)


def _pad_heads(w, per_head, left=0):
    rows = w.shape[0]
    w = w.reshape(rows, A_HEADS, per_head)
    w = jnp.pad(w, ((0, 0), (0, 0), (left, LANES - per_head - left)))
    return w.reshape(rows, HEAD_W)


def _const_spec(shape):
    return pl.BlockSpec(shape, lambda *_: (0,) * len(shape))


def kernel(x, positions, norm_in_g, w_in, q_lora_g, w_uq, kv_lora_g, w_ukv, q_head_g, k_head_g,
           v_gate_g, w_s, b_s, out_a_g, out_b_g, w_out):
    batch, seq, _ = x.shape
    n_tok = batch * seq
    assert seq % Q_TILE == 0 and seq % KV_TILE == 0 and seq % TOK_TILE == 0 and TOK_TILE % CHUNK == 0

    splits = np.cumsum([Q_LORA_RANK, KV_LORA_RANK, QK_ROPE_DIM, A_WIDTH, B_WIDTH, B_WIDTH])
    wi_cq, wi_ckv, wi_kr, wi_za, wi_u, wi_v, wi_zb = jnp.split(w_in, splits.tolist(), axis=1)
    wi_kr = jnp.pad(wi_kr, ((0, 0), (QK_NOPE_DIM, LANES - QK_DIM)))
    w_in_ext = jnp.concatenate([wi_cq, wi_ckv, wi_kr, wi_za, wi_u, wi_v, wi_zb], axis=1).astype(bf16)

    uq = w_uq.reshape(Q_LORA_RANK, A_HEADS, QK_DIM)
    uq_swapped = jnp.concatenate(
        [jnp.zeros_like(uq[..., :QK_NOPE_DIM]), uq[..., QK_NOPE_DIM + HALF_ROPE:], uq[..., QK_NOPE_DIM:QK_NOPE_DIM + HALF_ROPE]],
        axis=-1)
    w_uq_ext = jnp.concatenate([_pad_heads(uq.reshape(Q_LORA_RANK, -1), QK_DIM),
                                _pad_heads(uq_swapped.reshape(Q_LORA_RANK, -1), QK_DIM)], axis=1).astype(bf16)

    ukv = w_ukv.reshape(KV_LORA_RANK, A_HEADS, QK_NOPE_DIM + A_V_DIM)
    uk = _pad_heads(ukv[..., :QK_NOPE_DIM].reshape(KV_LORA_RANK, -1), QK_NOPE_DIM)
    uv = _pad_heads(ukv[..., QK_NOPE_DIM:].reshape(KV_LORA_RANK, -1), A_V_DIM)
    w_ukv_ext = jnp.concatenate([uk, uv], axis=1).astype(bf16)
    vone = np.zeros((1, LANES), np.float32)
    vone[0, A_V_DIM] = 1.0
    vone = jnp.asarray(vone)

    scale = math.log2(math.e) / math.sqrt(QK_DIM)
    qhg = jnp.pad(q_head_g * scale, (0, LANES - QK_DIM)).reshape(1, LANES)
    khg = jnp.pad(k_head_g, (0, LANES - QK_DIM)).reshape(1, LANES)

    inv_freq = 1.0 / (ROPE_THETA ** (jnp.arange(0, QK_ROPE_DIM, 2, dtype=f32) / QK_ROPE_DIM))
    invf = jnp.concatenate([jnp.zeros((QK_NOPE_DIM,), f32), inv_freq, inv_freq,
                            jnp.zeros((LANES - QK_DIM,), f32)]).reshape(1, LANES)
    sgn = np.zeros((1, LANES), np.float32)
    sgn[0, QK_NOPE_DIM:QK_NOPE_DIM + HALF_ROPE] = -1.0
    sgn[0, QK_NOPE_DIM + HALF_ROPE:QK_DIM] = 1.0
    sgn = jnp.asarray(sgn)

    head_of = np.arange(B_WIDTH) // B_HEAD_DIM
    ind = jnp.asarray((head_of[:, None] == head_of[None, :]).astype(np.float32)).astype(bf16)
    wpair = jnp.concatenate([w_s[0::2], w_s[1::2]], axis=2).astype(bf16)
    bpair = jnp.concatenate([jnp.broadcast_to(b_s[0::2, :, None], (B_HEADS // 2, CHUNK, B_HEAD_DIM)),
                             jnp.broadcast_to(b_s[1::2, :, None], (B_HEADS // 2, CHUNK, B_HEAD_DIM))], axis=2)

    x2 = x.reshape(n_tok, D_MODEL)
    pos2 = positions.reshape(n_tok, 1)
    tiles_per_row = seq // TOK_TILE

    head_out = lambda t: pl.BlockSpec((1, A_HEADS, t, LANES), lambda i: (i // tiles_per_row, 0, i % tiles_per_row, 0))
    q, k, v, ga, mb = pl.pallas_call(
        _proj_body,
        grid=(n_tok // TOK_TILE,),
        in_specs=[
            pl.BlockSpec((TOK_TILE, D_MODEL), lambda i: (i, 0)),
            pl.BlockSpec((TOK_TILE, 1), lambda i: (i, 0)),
            _const_spec((1, D_MODEL)),
            _const_spec((D_MODEL, D_IN_EXT)),
            _const_spec((1, Q_LORA_RANK)),
            _const_spec((Q_LORA_RANK, 2 * HEAD_W)),
            _const_spec((1, KV_LORA_RANK)),
            _const_spec((KV_LORA_RANK, 2 * HEAD_W)),
            _const_spec((1, LANES)),
            _const_spec((1, LANES)),
            _const_spec((1, B_WIDTH)),
            _const_spec((B_WIDTH, B_WIDTH)),
            _const_spec((B_HEADS // 2, CHUNK, 2 * CHUNK)),
            _const_spec((B_HEADS // 2, CHUNK, LANES)),
            _const_spec((1, B_WIDTH)),
            _const_spec((1, LANES)),
            _const_spec((1, LANES)),
            _const_spec((1, LANES)),
        ],
        out_specs=[
            head_out(TOK_TILE), head_out(TOK_TILE),
            pl.BlockSpec((1, A_HEADS, V_ROWS, TOK_TILE), lambda i: (i // tiles_per_row, 0, 0, i % tiles_per_row)),
            pl.BlockSpec((TOK_TILE, A_WIDTH), lambda i: (i, 0)),
            pl.BlockSpec((TOK_TILE, B_WIDTH), lambda i: (i, 0)),
        ],
        out_shape=[
            jax.ShapeDtypeStruct((batch, A_HEADS, seq, LANES), bf16),
            jax.ShapeDtypeStruct((batch, A_HEADS, seq, LANES), bf16),
            jax.ShapeDtypeStruct((batch, A_HEADS, V_ROWS, seq), bf16),
            jax.ShapeDtypeStruct((n_tok, A_WIDTH), bf16),
            jax.ShapeDtypeStruct((n_tok, B_WIDTH), bf16),
        ],
        compiler_params=pltpu.CompilerParams(dimension_semantics=("arbitrary",), vmem_limit_bytes=VMEM_LIMIT),
        name="proj",
    )(x2, pos2, norm_in_g.reshape(1, D_MODEL), w_in_ext, q_lora_g.reshape(1, -1), w_uq_ext,
      kv_lora_g.reshape(1, -1), w_ukv_ext, qhg, khg, v_gate_g.reshape(1, B_WIDTH), ind, wpair, bpair,
      out_b_g.reshape(1, B_WIDTH), invf, sgn, vone)

    q_tiles = seq // Q_TILE
    tok_blk = lambda w: pl.BlockSpec((Q_TILE, w), lambda b, i: (b * q_tiles + i, 0))
    y = pl.pallas_call(
        _attn_body,
        grid=(batch, q_tiles),
        in_specs=[
            pl.BlockSpec((1, A_HEADS, Q_TILE, LANES), lambda b, i: (b, 0, i, 0)),
            pl.BlockSpec((1, A_HEADS, seq, LANES), lambda b, i: (b, 0, 0, 0)),
            pl.BlockSpec((1, A_HEADS, V_ROWS, seq), lambda b, i: (b, 0, 0, 0)),
            tok_blk(A_WIDTH), tok_blk(B_WIDTH), tok_blk(D_MODEL),
            pl.BlockSpec((1, A_WIDTH), lambda b, i: (0, 0)),
            pl.BlockSpec((D_MODEL, D_MODEL), lambda b, i: (0, 0)),
        ],
        out_specs=tok_blk(D_MODEL),
        out_shape=jax.ShapeDtypeStruct((n_tok, D_MODEL), f32),
        scratch_shapes=[pltpu.VMEM((seq, Q_TILE), f32), pltpu.VMEM((seq, Q_TILE), f32),
                        pltpu.VMEM((A_HEADS, V_ROWS, Q_TILE), f32)],
        compiler_params=pltpu.CompilerParams(dimension_semantics=("arbitrary", "arbitrary"),
                                             vmem_limit_bytes=VMEM_LIMIT),
        name="attn",
    )(q, k, v, ga, mb, x2, out_a_g.reshape(1, A_WIDTH), w_out.astype(bf16))
    return y.reshape(batch, seq, D_MODEL)
```

```python
import math

import jax
import jax.numpy as jnp
import numpy as np
from jax import lax
from jax.experimental import pallas as pl
from jax.experimental.pallas import tpu as pltpu

D_MODEL = 1024
A_HEADS = 8
A_V_DIM = 64
A_WIDTH = A_HEADS * A_V_DIM
QK_NOPE_DIM = 64
QK_ROPE_DIM = 32
HALF_ROPE = QK_ROPE_DIM // 2
QK_DIM = QK_NOPE_DIM + QK_ROPE_DIM
Q_LORA_RANK = 256
KV_LORA_RANK = 128
ROPE_THETA = 10000.0
B_HEADS = 8
B_HEAD_DIM = 64
B_WIDTH = B_HEADS * B_HEAD_DIM
CHUNK = 128
EPS = 1e-6

LANES = 128
HEAD_W = A_HEADS * LANES
V_ROWS = 80

OFF_CQ = 0
OFF_CKV = OFF_CQ + Q_LORA_RANK
OFF_KR = OFF_CKV + KV_LORA_RANK
OFF_ZA = OFF_KR + LANES
OFF_U = OFF_ZA + A_WIDTH
OFF_V = OFF_U + B_WIDTH
OFF_ZB = OFF_V + B_WIDTH
D_IN_EXT = OFF_ZB + B_WIDTH

TOK_TILE = 512
ROPE_ROWS = TOK_TILE * QK_ROPE_DIM // LANES
Q_TILE = 256
KV_TILE = 512
VMEM_LIMIT = 56 * 1024 * 1024

f32 = jnp.float32
bf16 = jnp.bfloat16


def _gelu(t):
    return jax.nn.gelu(t)


def _silu(t):
    half = 0.5 * t
    return half + half * jnp.tanh(half)


def _rms(t, width):
    return lax.rsqrt(jnp.sum(t * t, axis=-1, keepdims=True) * (1.0 / width) + EPS)


def _proj_body(x_ref, pos_ref, g_in_ref, w_in_ref, gq_ref, w_uq_ref, gkv_ref, w_ukv_ref,
               qhg_ref, khg_ref, vgg_ref, ind_ref, wpair_ref, bpair_ref, gob_ref,
               invf_ref, sgn_ref, vone_ref,
               q_out, k_out, v_out, ga_out, mb_out):
    x = x_ref[...]
    h = (x * _rms(x, D_MODEL) * g_in_ref[...]).astype(bf16)

    def proj(lo, width):
        return jnp.dot(h, w_in_ref[:, lo:lo + width], preferred_element_type=f32)

    lane = lax.broadcasted_iota(jnp.int32, (1, LANES), 1)
    rope_lanes = (lane >= QK_NOPE_DIM) & (lane < QK_DIM)
    ang = pos_ref[...].astype(f32) * invf_ref[...]
    cos_c, sin_c = jnp.cos(ang), jnp.sin(ang)
    cos_rows, sin_rows = [], []
    for g in range(LANES // QK_ROPE_DIM):
        shift = (QK_NOPE_DIM - QK_ROPE_DIM * g) % LANES
        cos_rows.append(pltpu.roll(cos_c, shift, 1) if shift else cos_c)
        sin_rows.append(pltpu.roll(sin_c, shift, 1) if shift else sin_c)
    cos = jnp.where(rope_lanes, jnp.concatenate(cos_rows, axis=0), 1.0)
    sin_signed = jnp.concatenate(sin_rows, axis=0) * sgn_ref[...]

    c_q = proj(OFF_CQ, Q_LORA_RANK)
    cqn = (c_q * _rms(c_q, Q_LORA_RANK) * gq_ref[...]).astype(bf16)
    qhg = qhg_ref[...]
    for hd in range(A_HEADS):
        qq = jnp.dot(cqn, w_uq_ref[:, 2 * hd * LANES:2 * (hd + 1) * LANES], preferred_element_type=f32)
        qh = qq[:, :LANES] * cos + qq[:, LANES:] * sin_signed
        q_out[0, hd] = (qh * _rms(qh, QK_DIM) * qhg).astype(bf16)

    c_kv = proj(OFF_CKV, KV_LORA_RANK)
    ckvn = (c_kv * _rms(c_kv, KV_LORA_RANK) * gkv_ref[...]).astype(bf16)
    kr = proj(OFF_KR, LANES)
    first = (lane >= QK_NOPE_DIM) & (lane < QK_NOPE_DIM + HALF_ROPE)
    kr_swapped = jnp.where(first, pltpu.roll(kr, LANES - HALF_ROPE, 1), pltpu.roll(kr, HALF_ROPE, 1))
    kpe = jnp.where(rope_lanes, kr * cos + kr_swapped * sin_signed, 0.0)
    khg = khg_ref[...]
    for hd in range(A_HEADS):
        kv = jnp.dot(ckvn, w_ukv_ref[:, 2 * hd * LANES:2 * (hd + 1) * LANES], preferred_element_type=f32)
        kh = kv[:, :LANES] + kpe
        k_out[0, hd] = (kh * _rms(kh, QK_DIM) * khg).astype(bf16)
        vh = kv[:, LANES:] + vone_ref[...]
        v_out[0, hd] = vh.T[0:V_ROWS].astype(bf16)

    ga_out[...] = _silu(proj(OFF_ZA, A_WIDTH)).astype(bf16)

    u = _gelu(proj(OFF_U, B_WIDTH))
    v = _gelu(proj(OFF_V, B_WIDTH))
    vv = v * v
    vv_hi = vv.astype(bf16)
    vv_lo = (vv - vv_hi.astype(f32)).astype(bf16)
    ind = ind_ref[...]
    ssq = (jnp.dot(vv_hi, ind, preferred_element_type=f32) + jnp.dot(vv_lo, ind, preferred_element_type=f32))
    vn = (v * lax.rsqrt(ssq * (1.0 / B_HEAD_DIM) + EPS) * vgg_ref[...]).astype(bf16)
    lower = lane < B_HEAD_DIM
    zero = jnp.zeros((), bf16)
    chunks = []
    for c in range(TOK_TILE // CHUNK):
        pairs = []
        for p in range(B_HEADS // 2):
            vp = vn[c * CHUNK:(c + 1) * CHUNK, p * LANES:(p + 1) * LANES]
            rhs = jnp.concatenate([jnp.where(lower, vp, zero), jnp.where(lower, zero, vp)], axis=0)
            sv = jnp.dot(wpair_ref[p], rhs, preferred_element_type=f32) + bpair_ref[p]
            pairs.append(u[c * CHUNK:(c + 1) * CHUNK, p * LANES:(p + 1) * LANES] * sv)
        chunks.append(jnp.concatenate(pairs, axis=1))
    ob = jnp.concatenate(chunks, axis=0)
    obn = ob * _rms(ob, B_WIDTH) * gob_ref[...]
    mb_out[...] = (obn * _silu(proj(OFF_ZB, B_WIDTH))).astype(bf16)


def _attn_body(q_ref, k_ref, v_ref, ga_ref, mb_ref, x_ref, goa_ref, w_out_ref, y_ref,
               s_even, s_odd, acc_scr):
    seq = k_ref.shape[2]
    n_kv = seq // KV_TILE

    def head_step(hd, parity, m_prev, do_scores=True, do_weighted=True):
        s_cur, s_prev = (s_even, s_odd) if parity == 0 else (s_odd, s_even)
        q_t = q_ref[0, hd].T if do_scores else None
        m = acc = None
        for t in range(n_kv):
            keys = slice(t * KV_TILE, (t + 1) * KV_TILE)
            if do_scores:
                s = jnp.dot(k_ref[0, hd, keys, :], q_t, preferred_element_type=f32)
                s_cur[keys, :] = s
                m = s if m is None else jnp.maximum(m, s)
            if do_weighted:
                p = jnp.exp2(s_prev[keys, :] - m_prev).astype(bf16)
                pv = jnp.dot(v_ref[0, hd - 1, :, keys], p, preferred_element_type=f32)
                acc = pv if acc is None else acc + pv
        if do_weighted:
            acc_scr[hd - 1] = acc
        return jnp.max(m, axis=0, keepdims=True) if do_scores else None

    def pair_step(i, m_prev):
        m_odd = head_step(2 * i + 1, 1, m_prev)
        return head_step(2 * i + 2, 0, m_odd)

    m_cur = head_step(0, 0, None, do_weighted=False)
    m_cur = lax.fori_loop(0, (A_HEADS - 2) // 2, pair_step, m_cur)
    m_cur = head_step(A_HEADS - 1, 1, m_cur)
    half = D_MODEL // 2
    y_b = [x_ref[:, c:c + half] + jnp.dot(mb_ref[...], w_out_ref[A_WIDTH:, c:c + half], preferred_element_type=f32)
           for c in (0, half)]
    head_step(A_HEADS, 0, m_cur, do_scores=False)

    heads = []
    for hd in range(A_HEADS):
        acc = acc_scr[hd]
        heads.append(acc[0:A_V_DIM] / acc[A_V_DIM:A_V_DIM + 1])
    o = jnp.concatenate(heads, axis=0).T
    on = o * lax.rsqrt(jnp.mean(o * o, axis=-1, keepdims=True) + EPS) * goa_ref[...]
    mix_a = (on * ga_ref[...].astype(f32)).astype(bf16)
    for i, c in enumerate((0, half)):
        y_ref[:, c:c + half] = y_b[i] + jnp.dot(mix_a, w_out_ref[0:A_WIDTH, c:c + half], preferred_element_type=f32)


def _pad_heads(w, per_head, left=0):
    rows = w.shape[0]
    w = w.reshape(rows, A_HEADS, per_head)
    w = jnp.pad(w, ((0, 0), (0, 0), (left, LANES - per_head - left)))
    return w.reshape(rows, HEAD_W)


def _interleave_heads(a, b):
    rows = a.shape[0]
    return jnp.stack([a.reshape(rows, A_HEADS, LANES), b.reshape(rows, A_HEADS, LANES)], axis=2).reshape(rows, 2 * HEAD_W)


def _const_spec(shape):
    return pl.BlockSpec(shape, lambda *_: (0,) * len(shape))


def kernel(x, positions, norm_in_g, w_in, q_lora_g, w_uq, kv_lora_g, w_ukv, q_head_g, k_head_g,
           v_gate_g, w_s, b_s, out_a_g, out_b_g, w_out):
    batch, seq, _ = x.shape
    n_tok = batch * seq
    assert seq % Q_TILE == 0 and seq % KV_TILE == 0 and seq % TOK_TILE == 0 and TOK_TILE % CHUNK == 0

    splits = np.cumsum([Q_LORA_RANK, KV_LORA_RANK, QK_ROPE_DIM, A_WIDTH, B_WIDTH, B_WIDTH])
    wi_cq, wi_ckv, wi_kr, wi_za, wi_u, wi_v, wi_zb = jnp.split(w_in, splits.tolist(), axis=1)
    wi_kr = jnp.pad(wi_kr, ((0, 0), (QK_NOPE_DIM, LANES - QK_DIM)))
    w_in_ext = jnp.concatenate([wi_cq, wi_ckv, wi_kr, wi_za, wi_u, wi_v, wi_zb], axis=1).astype(bf16)

    uq = w_uq.reshape(Q_LORA_RANK, A_HEADS, QK_DIM)
    uq_swapped = jnp.concatenate(
        [jnp.zeros_like(uq[..., :QK_NOPE_DIM]), uq[..., QK_NOPE_DIM + HALF_ROPE:], uq[..., QK_NOPE_DIM:QK_NOPE_DIM + HALF_ROPE]],
        axis=-1)
    w_uq_ext = _interleave_heads(_pad_heads(uq.reshape(Q_LORA_RANK, -1), QK_DIM),
                                 _pad_heads(uq_swapped.reshape(Q_LORA_RANK, -1), QK_DIM)).astype(bf16)

    ukv = w_ukv.reshape(KV_LORA_RANK, A_HEADS, QK_NOPE_DIM + A_V_DIM)
    uk = _pad_heads(ukv[..., :QK_NOPE_DIM].reshape(KV_LORA_RANK, -1), QK_NOPE_DIM)
    uv = _pad_heads(ukv[..., QK_NOPE_DIM:].reshape(KV_LORA_RANK, -1), A_V_DIM)
    w_ukv_ext = _interleave_heads(uk, uv).astype(bf16)
    vone = np.zeros((1, LANES), np.float32)
    vone[0, A_V_DIM] = 1.0
    vone = jnp.asarray(vone)

    scale = math.log2(math.e) / math.sqrt(QK_DIM)
    qhg = jnp.pad(q_head_g * scale, (0, LANES - QK_DIM)).reshape(1, LANES)
    khg = jnp.pad(k_head_g, (0, LANES - QK_DIM)).reshape(1, LANES)

    inv_freq = 1.0 / (ROPE_THETA ** (jnp.arange(0, QK_ROPE_DIM, 2, dtype=f32) / QK_ROPE_DIM))
    invf = jnp.tile(inv_freq, LANES // HALF_ROPE).reshape(1, LANES)
    sgn = np.zeros((1, LANES), np.float32)
    sgn[0, QK_NOPE_DIM:QK_NOPE_DIM + HALF_ROPE] = -1.0
    sgn[0, QK_NOPE_DIM + HALF_ROPE:QK_DIM] = 1.0
    sgn = jnp.asarray(sgn)

    head_of = np.arange(B_WIDTH) // B_HEAD_DIM
    ind = jnp.asarray((head_of[:, None] == head_of[None, :]).astype(np.float32)).astype(bf16)
    wpair = jnp.concatenate([w_s[0::2], w_s[1::2]], axis=2).astype(bf16)
    bpair = jnp.concatenate([jnp.broadcast_to(b_s[0::2, :, None], (B_HEADS // 2, CHUNK, B_HEAD_DIM)),
                             jnp.broadcast_to(b_s[1::2, :, None], (B_HEADS // 2, CHUNK, B_HEAD_DIM))], axis=2)

    x2 = x.reshape(n_tok, D_MODEL)
    groups = LANES // QK_ROPE_DIM
    pos2 = positions.reshape(n_tok // TOK_TILE, groups, ROPE_ROWS).transpose(0, 2, 1)
    pos2 = jnp.repeat(pos2, QK_ROPE_DIM, axis=2).reshape(n_tok // groups, LANES)
    tiles_per_row = seq // TOK_TILE

    head_out = lambda t: pl.BlockSpec((1, A_HEADS, t, LANES), lambda i: (i // tiles_per_row, 0, i % tiles_per_row, 0))
    q, k, v, ga, mb = pl.pallas_call(
        _proj_body,
        grid=(n_tok // TOK_TILE,),
        in_specs=[
            pl.BlockSpec((TOK_TILE, D_MODEL), lambda i: (i, 0)),
            pl.BlockSpec((ROPE_ROWS, LANES), lambda i: (i, 0)),
            _const_spec((1, D_MODEL)),
            _const_spec((D_MODEL, D_IN_EXT)),
            _const_spec((1, Q_LORA_RANK)),
            _const_spec((Q_LORA_RANK, 2 * HEAD_W)),
            _const_spec((1, KV_LORA_RANK)),
            _const_spec((KV_LORA_RANK, 2 * HEAD_W)),
            _const_spec((1, LANES)),
            _const_spec((1, LANES)),
            _const_spec((1, B_WIDTH)),
            _const_spec((B_WIDTH, B_WIDTH)),
            _const_spec((B_HEADS // 2, CHUNK, 2 * CHUNK)),
            _const_spec((B_HEADS // 2, CHUNK, LANES)),
            _const_spec((1, B_WIDTH)),
            _const_spec((1, LANES)),
            _const_spec((1, LANES)),
            _const_spec((1, LANES)),
        ],
        out_specs=[
            head_out(TOK_TILE), head_out(TOK_TILE),
            pl.BlockSpec((1, A_HEADS, V_ROWS, TOK_TILE), lambda i: (i // tiles_per_row, 0, 0, i % tiles_per_row)),
            pl.BlockSpec((TOK_TILE, A_WIDTH), lambda i: (i, 0)),
            pl.BlockSpec((TOK_TILE, B_WIDTH), lambda i: (i, 0)),
        ],
        out_shape=[
            jax.ShapeDtypeStruct((batch, A_HEADS, seq, LANES), bf16),
            jax.ShapeDtypeStruct((batch, A_HEADS, seq, LANES), bf16),
            jax.ShapeDtypeStruct((batch, A_HEADS, V_ROWS, seq), bf16),
            jax.ShapeDtypeStruct((n_tok, A_WIDTH), bf16),
            jax.ShapeDtypeStruct((n_tok, B_WIDTH), bf16),
        ],
        compiler_params=pltpu.CompilerParams(dimension_semantics=("arbitrary",), vmem_limit_bytes=VMEM_LIMIT),
        name="proj",
    )(x2, pos2, norm_in_g.reshape(1, D_MODEL), w_in_ext, q_lora_g.reshape(1, -1), w_uq_ext,
      kv_lora_g.reshape(1, -1), w_ukv_ext, qhg, khg, v_gate_g.reshape(1, B_WIDTH), ind, wpair, bpair,
      out_b_g.reshape(1, B_WIDTH), invf, sgn, vone)

    q_tiles = seq // Q_TILE
    tok_blk = lambda w: pl.BlockSpec((Q_TILE, w), lambda b, i: (b * q_tiles + i, 0))
    y = pl.pallas_call(
        _attn_body,
        grid=(batch, q_tiles),
        in_specs=[
            pl.BlockSpec((1, A_HEADS, Q_TILE, LANES), lambda b, i: (b, 0, i, 0)),
            pl.BlockSpec((1, A_HEADS, seq, LANES), lambda b, i: (b, 0, 0, 0)),
            pl.BlockSpec((1, A_HEADS, V_ROWS, seq), lambda b, i: (b, 0, 0, 0)),
            tok_blk(A_WIDTH), tok_blk(B_WIDTH), tok_blk(D_MODEL),
            pl.BlockSpec((1, A_WIDTH), lambda b, i: (0, 0)),
            pl.BlockSpec((D_MODEL, D_MODEL), lambda b, i: (0, 0)),
        ],
        out_specs=tok_blk(D_MODEL),
        out_shape=jax.ShapeDtypeStruct((n_tok, D_MODEL), f32),
        scratch_shapes=[pltpu.VMEM((seq, Q_TILE), f32), pltpu.VMEM((seq, Q_TILE), f32),
                        pltpu.VMEM((A_HEADS, V_ROWS, Q_TILE), f32)],
        compiler_params=pltpu.CompilerParams(dimension_semantics=("arbitrary", "arbitrary"),
                                             vmem_limit_bytes=VMEM_LIMIT),
        name="attn",
    )(q, k, v, ga, mb, x2, out_a_g.reshape(1, A_WIDTH), w_out.astype(bf16))
    return y.reshape(batch, seq, D_MODEL)
```

```python
import math

import jax
import jax.numpy as jnp
import numpy as np
from jax import lax
from jax.experimental import pallas as pl
from jax.experimental.pallas import tpu as pltpu

D_MODEL = 1024
A_HEADS = 8
A_V_DIM = 64
A_WIDTH = A_HEADS * A_V_DIM
QK_NOPE_DIM = 64
QK_ROPE_DIM = 32
HALF_ROPE = QK_ROPE_DIM // 2
QK_DIM = QK_NOPE_DIM + QK_ROPE_DIM
Q_LORA_RANK = 256
KV_LORA_RANK = 128
ROPE_THETA = 10000.0
B_HEADS = 8
B_HEAD_DIM = 64
B_WIDTH = B_HEADS * B_HEAD_DIM
CHUNK = 128
EPS = 1e-6

LANES = 128
HEAD_W = A_HEADS * LANES
V_ROWS = 80

OFF_CQ = 0
OFF_CKV = OFF_CQ + Q_LORA_RANK
OFF_KR = OFF_CKV + KV_LORA_RANK
OFF_ZA = OFF_KR + LANES
OFF_U = OFF_ZA + A_WIDTH
OFF_V = OFF_U + B_WIDTH
OFF_ZB = OFF_V + B_WIDTH
D_IN_EXT = OFF_ZB + B_WIDTH

TOK_TILE = 512
ROPE_ROWS = TOK_TILE * QK_ROPE_DIM // LANES
Q_TILE = 512
KV_TILE = 512
VMEM_LIMIT = 56 * 1024 * 1024

f32 = jnp.float32
bf16 = jnp.bfloat16


def _gelu(t):
    return jax.nn.gelu(t)


def _silu(t):
    half = 0.5 * t
    return half + half * jnp.tanh(half)


def _rms(t, width):
    return lax.rsqrt(jnp.sum(t * t, axis=-1, keepdims=True) * (1.0 / width) + EPS)


def _proj_body(x_ref, pos_ref, g_in_ref, w_in_ref, gq_ref, w_uq_ref, gkv_ref, w_ukv_ref,
               qhg_ref, khg_ref, vgg_ref, ind_ref, wpair_ref, bpair_ref, gob_ref,
               invf_ref, sgn_ref, vone_ref,
               q_out, k_out, v_out, ga_out, mb_out):
    x = x_ref[...]
    h = (x * _rms(x, D_MODEL) * g_in_ref[...]).astype(bf16)

    def proj(lo, width):
        return jnp.dot(h, w_in_ref[:, lo:lo + width], preferred_element_type=f32)

    lane = lax.broadcasted_iota(jnp.int32, (1, LANES), 1)
    rope_lanes = (lane >= QK_NOPE_DIM) & (lane < QK_DIM)
    ang = pos_ref[...].astype(f32) * invf_ref[...]
    cos_c, sin_c = jnp.cos(ang), jnp.sin(ang)
    cos_rows, sin_rows = [], []
    for g in range(LANES // QK_ROPE_DIM):
        shift = (QK_NOPE_DIM - QK_ROPE_DIM * g) % LANES
        cos_rows.append(pltpu.roll(cos_c, shift, 1) if shift else cos_c)
        sin_rows.append(pltpu.roll(sin_c, shift, 1) if shift else sin_c)
    cos = jnp.where(rope_lanes, jnp.concatenate(cos_rows, axis=0), 1.0)
    sin_signed = jnp.concatenate(sin_rows, axis=0) * sgn_ref[...]

    c_q = proj(OFF_CQ, Q_LORA_RANK)
    cqn = (c_q * _rms(c_q, Q_LORA_RANK) * gq_ref[...]).astype(bf16)
    qhg = qhg_ref[...]
    for hd in range(A_HEADS):
        qq = jnp.dot(cqn, w_uq_ref[:, 2 * hd * LANES:2 * (hd + 1) * LANES], preferred_element_type=f32)
        qh = qq[:, :LANES] * cos + qq[:, LANES:] * sin_signed
        q_out[0, hd] = (qh * _rms(qh, QK_DIM) * qhg).astype(bf16)

    c_kv = proj(OFF_CKV, KV_LORA_RANK)
    ckvn = (c_kv * _rms(c_kv, KV_LORA_RANK) * gkv_ref[...]).astype(bf16)
    kr = proj(OFF_KR, LANES)
    first = (lane >= QK_NOPE_DIM) & (lane < QK_NOPE_DIM + HALF_ROPE)
    kr_swapped = jnp.where(first, pltpu.roll(kr, LANES - HALF_ROPE, 1), pltpu.roll(kr, HALF_ROPE, 1))
    kpe = jnp.where(rope_lanes, kr * cos + kr_swapped * sin_signed, 0.0)
    khg = khg_ref[...]
    for hd in range(A_HEADS):
        kv = jnp.dot(ckvn, w_ukv_ref[:, 2 * hd * LANES:2 * (hd + 1) * LANES], preferred_element_type=f32)
        kh = kv[:, :LANES] + kpe
        k_out[0, hd] = (kh * _rms(kh, QK_DIM) * khg).astype(bf16)
        vh = kv[:, LANES:] + vone_ref[...]
        v_out[0, hd] = vh.T[0:V_ROWS].astype(bf16)

    ga_out[...] = _silu(proj(OFF_ZA, A_WIDTH)).astype(bf16)

    u = _gelu(proj(OFF_U, B_WIDTH))
    v = _gelu(proj(OFF_V, B_WIDTH))
    vv = v * v
    vv_hi = vv.astype(bf16)
    vv_lo = (vv - vv_hi.astype(f32)).astype(bf16)
    ind = ind_ref[...]
    ssq = (jnp.dot(vv_hi, ind, preferred_element_type=f32) + jnp.dot(vv_lo, ind, preferred_element_type=f32))
    vn = (v * lax.rsqrt(ssq * (1.0 / B_HEAD_DIM) + EPS) * vgg_ref[...]).astype(bf16)
    lower = lane < B_HEAD_DIM
    zero = jnp.zeros((), bf16)
    chunks = []
    for c in range(TOK_TILE // CHUNK):
        pairs = []
        for p in range(B_HEADS // 2):
            vp = vn[c * CHUNK:(c + 1) * CHUNK, p * LANES:(p + 1) * LANES]
            rhs = jnp.concatenate([jnp.where(lower, vp, zero), jnp.where(lower, zero, vp)], axis=0)
            sv = jnp.dot(wpair_ref[p], rhs, preferred_element_type=f32) + bpair_ref[p]
            pairs.append(u[c * CHUNK:(c + 1) * CHUNK, p * LANES:(p + 1) * LANES] * sv)
        chunks.append(jnp.concatenate(pairs, axis=1))
    ob = jnp.concatenate(chunks, axis=0)
    obn = ob * _rms(ob, B_WIDTH) * gob_ref[...]
    mb_out[...] = (obn * _silu(proj(OFF_ZB, B_WIDTH))).astype(bf16)


def _attn_body(q_ref, k_ref, v_ref, ga_ref, mb_ref, x_ref, goa_ref, w_out_ref, y_ref,
               s_even, s_odd, acc_scr):
    seq = k_ref.shape[2]
    n_kv = seq // KV_TILE

    def head_step(hd, parity, m_prev, do_scores=True, do_weighted=True):
        s_cur, s_prev = (s_even, s_odd) if parity == 0 else (s_odd, s_even)
        q_t = q_ref[0, hd].T if do_scores else None
        m = None
        acc = [None, None]
        for t in range(n_kv):
            keys = slice(t * KV_TILE, (t + 1) * KV_TILE)
            if do_scores:
                s = jnp.dot(k_ref[0, hd, keys, :], q_t, preferred_element_type=f32)
                s_cur[keys, :] = s
                m = s if m is None else jnp.maximum(m, s)
            if do_weighted:
                p = jnp.exp2(s_prev[keys, :] - m_prev).astype(bf16)
                pv = jnp.dot(v_ref[0, hd - 1, :, keys], p, preferred_element_type=f32)
                acc[t % 2] = pv if acc[t % 2] is None else acc[t % 2] + pv
        if do_weighted:
            acc_scr[hd - 1] = acc[0] + acc[1]
        return jnp.max(m, axis=0, keepdims=True) if do_scores else None

    def pair_step(i, m_prev):
        m_odd = head_step(2 * i + 1, 1, m_prev)
        return head_step(2 * i + 2, 0, m_odd)

    m_cur = head_step(0, 0, None, do_weighted=False)
    m_cur = lax.fori_loop(0, (A_HEADS - 2) // 2, pair_step, m_cur)
    m_cur = head_step(A_HEADS - 1, 1, m_cur)
    half = D_MODEL // 2
    y_b = [x_ref[:, c:c + half] + jnp.dot(mb_ref[...], w_out_ref[A_WIDTH:, c:c + half], preferred_element_type=f32)
           for c in (0, half)]
    head_step(A_HEADS, 0, m_cur, do_scores=False)

    heads = []
    for hd in range(A_HEADS):
        acc = acc_scr[hd]
        heads.append(acc[0:A_V_DIM] / acc[A_V_DIM:A_V_DIM + 1])
    o = jnp.concatenate(heads, axis=0).T
    on = o * lax.rsqrt(jnp.mean(o * o, axis=-1, keepdims=True) + EPS) * goa_ref[...]
    mix_a = (on * ga_ref[...].astype(f32)).astype(bf16)
    for i, c in enumerate((0, half)):
        y_ref[:, c:c + half] = y_b[i] + jnp.dot(mix_a, w_out_ref[0:A_WIDTH, c:c + half], preferred_element_type=f32)


def _pad_heads(w, per_head, left=0):
    rows = w.shape[0]
    w = w.reshape(rows, A_HEADS, per_head)
    w = jnp.pad(w, ((0, 0), (0, 0), (left, LANES - per_head - left)))
    return w.reshape(rows, HEAD_W)


def _interleave_heads(a, b):
    rows = a.shape[0]
    return jnp.stack([a.reshape(rows, A_HEADS, LANES), b.reshape(rows, A_HEADS, LANES)], axis=2).reshape(rows, 2 * HEAD_W)


def _const_spec(shape):
    return pl.BlockSpec(shape, lambda *_: (0,) * len(shape))


def kernel(x, positions, norm_in_g, w_in, q_lora_g, w_uq, kv_lora_g, w_ukv, q_head_g, k_head_g,
           v_gate_g, w_s, b_s, out_a_g, out_b_g, w_out):
    batch, seq, _ = x.shape
    n_tok = batch * seq
    assert seq % Q_TILE == 0 and seq % KV_TILE == 0 and seq % TOK_TILE == 0 and TOK_TILE % CHUNK == 0

    splits = np.cumsum([Q_LORA_RANK, KV_LORA_RANK, QK_ROPE_DIM, A_WIDTH, B_WIDTH, B_WIDTH])
    wi_cq, wi_ckv, wi_kr, wi_za, wi_u, wi_v, wi_zb = jnp.split(w_in, splits.tolist(), axis=1)
    wi_kr = jnp.pad(wi_kr, ((0, 0), (QK_NOPE_DIM, LANES - QK_DIM)))
    w_in_ext = jnp.concatenate([wi_cq, wi_ckv, wi_kr, wi_za, wi_u, wi_v, wi_zb], axis=1).astype(bf16)

    uq = w_uq.reshape(Q_LORA_RANK, A_HEADS, QK_DIM)
    uq_swapped = jnp.concatenate(
        [jnp.zeros_like(uq[..., :QK_NOPE_DIM]), uq[..., QK_NOPE_DIM + HALF_ROPE:], uq[..., QK_NOPE_DIM:QK_NOPE_DIM + HALF_ROPE]],
        axis=-1)
    w_uq_ext = _interleave_heads(_pad_heads(uq.reshape(Q_LORA_RANK, -1), QK_DIM),
                                 _pad_heads(uq_swapped.reshape(Q_LORA_RANK, -1), QK_DIM)).astype(bf16)

    ukv = w_ukv.reshape(KV_LORA_RANK, A_HEADS, QK_NOPE_DIM + A_V_DIM)
    uk = _pad_heads(ukv[..., :QK_NOPE_DIM].reshape(KV_LORA_RANK, -1), QK_NOPE_DIM)
    uv = _pad_heads(ukv[..., QK_NOPE_DIM:].reshape(KV_LORA_RANK, -1), A_V_DIM)
    w_ukv_ext = _interleave_heads(uk, uv).astype(bf16)
    vone = np.zeros((1, LANES), np.float32)
    vone[0, A_V_DIM] = 1.0
    vone = jnp.asarray(vone)

    scale = math.log2(math.e) / math.sqrt(QK_DIM)
    qhg = jnp.pad(q_head_g * scale, (0, LANES - QK_DIM)).reshape(1, LANES)
    khg = jnp.pad(k_head_g, (0, LANES - QK_DIM)).reshape(1, LANES)

    inv_freq = 1.0 / (ROPE_THETA ** (jnp.arange(0, QK_ROPE_DIM, 2, dtype=f32) / QK_ROPE_DIM))
    invf = jnp.tile(inv_freq, LANES // HALF_ROPE).reshape(1, LANES)
    sgn = np.zeros((1, LANES), np.float32)
    sgn[0, QK_NOPE_DIM:QK_NOPE_DIM + HALF_ROPE] = -1.0
    sgn[0, QK_NOPE_DIM + HALF_ROPE:QK_DIM] = 1.0
    sgn = jnp.asarray(sgn)

    head_of = np.arange(B_WIDTH) // B_HEAD_DIM
    ind = jnp.asarray((head_of[:, None] == head_of[None, :]).astype(np.float32)).astype(bf16)
    wpair = jnp.concatenate([w_s[0::2], w_s[1::2]], axis=2).astype(bf16)
    bpair = jnp.concatenate([jnp.broadcast_to(b_s[0::2, :, None], (B_HEADS // 2, CHUNK, B_HEAD_DIM)),
                             jnp.broadcast_to(b_s[1::2, :, None], (B_HEADS // 2, CHUNK, B_HEAD_DIM))], axis=2)

    x2 = x.reshape(n_tok, D_MODEL)
    groups = LANES // QK_ROPE_DIM
    pos2 = positions.reshape(n_tok // TOK_TILE, groups, ROPE_ROWS).transpose(0, 2, 1)
    pos2 = jnp.repeat(pos2, QK_ROPE_DIM, axis=2).reshape(n_tok // groups, LANES)
    tiles_per_row = seq // TOK_TILE

    head_out = lambda t: pl.BlockSpec((1, A_HEADS, t, LANES), lambda i: (i // tiles_per_row, 0, i % tiles_per_row, 0))
    q, k, v, ga, mb = pl.pallas_call(
        _proj_body,
        grid=(n_tok // TOK_TILE,),
        in_specs=[
            pl.BlockSpec((TOK_TILE, D_MODEL), lambda i: (i, 0)),
            pl.BlockSpec((ROPE_ROWS, LANES), lambda i: (i, 0)),
            _const_spec((1, D_MODEL)),
            _const_spec((D_MODEL, D_IN_EXT)),
            _const_spec((1, Q_LORA_RANK)),
            _const_spec((Q_LORA_RANK, 2 * HEAD_W)),
            _const_spec((1, KV_LORA_RANK)),
            _const_spec((KV_LORA_RANK, 2 * HEAD_W)),
            _const_spec((1, LANES)),
            _const_spec((1, LANES)),
            _const_spec((1, B_WIDTH)),
            _const_spec((B_WIDTH, B_WIDTH)),
            _const_spec((B_HEADS // 2, CHUNK, 2 * CHUNK)),
            _const_spec((B_HEADS // 2, CHUNK, LANES)),
            _const_spec((1, B_WIDTH)),
            _const_spec((1, LANES)),
            _const_spec((1, LANES)),
            _const_spec((1, LANES)),
        ],
        out_specs=[
            head_out(TOK_TILE), head_out(TOK_TILE),
            pl.BlockSpec((1, A_HEADS, V_ROWS, TOK_TILE), lambda i: (i // tiles_per_row, 0, 0, i % tiles_per_row)),
            pl.BlockSpec((TOK_TILE, A_WIDTH), lambda i: (i, 0)),
            pl.BlockSpec((TOK_TILE, B_WIDTH), lambda i: (i, 0)),
        ],
        out_shape=[
            jax.ShapeDtypeStruct((batch, A_HEADS, seq, LANES), bf16),
            jax.ShapeDtypeStruct((batch, A_HEADS, seq, LANES), bf16),
            jax.ShapeDtypeStruct((batch, A_HEADS, V_ROWS, seq), bf16),
            jax.ShapeDtypeStruct((n_tok, A_WIDTH), bf16),
            jax.ShapeDtypeStruct((n_tok, B_WIDTH), bf16),
        ],
        compiler_params=pltpu.CompilerParams(dimension_semantics=("arbitrary",), vmem_limit_bytes=VMEM_LIMIT),
        name="proj",
    )(x2, pos2, norm_in_g.reshape(1, D_MODEL), w_in_ext, q_lora_g.reshape(1, -1), w_uq_ext,
      kv_lora_g.reshape(1, -1), w_ukv_ext, qhg, khg, v_gate_g.reshape(1, B_WIDTH), ind, wpair, bpair,
      out_b_g.reshape(1, B_WIDTH), invf, sgn, vone)

    q_tiles = seq // Q_TILE
    tok_blk = lambda w: pl.BlockSpec((Q_TILE, w), lambda b, i: (b * q_tiles + i, 0))
    resident = pl.Buffered(1)
    y = pl.pallas_call(
        _attn_body,
        grid=(batch, q_tiles),
        in_specs=[
            pl.BlockSpec((1, A_HEADS, Q_TILE, LANES), lambda b, i: (b, 0, i, 0)),
            pl.BlockSpec((1, A_HEADS, seq, LANES), lambda b, i: (b, 0, 0, 0), pipeline_mode=resident),
            pl.BlockSpec((1, A_HEADS, V_ROWS, seq), lambda b, i: (b, 0, 0, 0), pipeline_mode=resident),
            tok_blk(A_WIDTH), tok_blk(B_WIDTH), tok_blk(D_MODEL),
            pl.BlockSpec((1, A_WIDTH), lambda b, i: (0, 0), pipeline_mode=resident),
            pl.BlockSpec((D_MODEL, D_MODEL), lambda b, i: (0, 0), pipeline_mode=resident),
        ],
        out_specs=tok_blk(D_MODEL),
        out_shape=jax.ShapeDtypeStruct((n_tok, D_MODEL), f32),
        scratch_shapes=[pltpu.VMEM((seq, Q_TILE), f32), pltpu.VMEM((seq, Q_TILE), f32),
                        pltpu.VMEM((A_HEADS, V_ROWS, Q_TILE), f32)],
        compiler_params=pltpu.CompilerParams(dimension_semantics=("arbitrary", "arbitrary"),
                                             vmem_limit_bytes=VMEM_LIMIT),
        name="attn",
    )(q, k, v, ga, mb, x2, out_a_g.reshape(1, A_WIDTH), w_out.astype(bf16))
    return y.reshape(batch, seq, D_MODEL)
```

```python
import math

import jax
import jax.numpy as jnp
import numpy as np
from jax import lax
from jax.experimental import pallas as pl
from jax.experimental.pallas import tpu as pltpu

D_MODEL = 1024
A_HEADS = 8
A_V_DIM = 64
A_WIDTH = A_HEADS * A_V_DIM
QK_NOPE_DIM = 64
QK_ROPE_DIM = 32
HALF_ROPE = QK_ROPE_DIM // 2
QK_DIM = QK_NOPE_DIM + QK_ROPE_DIM
Q_LORA_RANK = 256
KV_LORA_RANK = 128
ROPE_THETA = 10000.0
B_HEADS = 8
B_HEAD_DIM = 64
B_WIDTH = B_HEADS * B_HEAD_DIM
CHUNK = 128
EPS = 1e-6

LANES = 128
HEAD_W = A_HEADS * LANES
V_ROWS = 80

OFF_CQ = 0
OFF_CKV = OFF_CQ + Q_LORA_RANK
OFF_KR = OFF_CKV + KV_LORA_RANK
OFF_ZA = OFF_KR + LANES
OFF_U = OFF_ZA + A_WIDTH
OFF_V = OFF_U + B_WIDTH
OFF_ZB = OFF_V + B_WIDTH
D_IN_EXT = OFF_ZB + B_WIDTH

TOK_TILE = 512
W_T_BLOCK = 256
ROPE_ROWS = TOK_TILE * QK_ROPE_DIM // LANES
Q_TILE = 512
KV_TILE = 1024
VMEM_LIMIT = 56 * 1024 * 1024

f32 = jnp.float32
bf16 = jnp.bfloat16


def _gelu(t):
    return jax.nn.gelu(t)


def _silu(t):
    half = 0.5 * t
    return half + half * jnp.tanh(half)


def _rms(t, width):
    return lax.rsqrt(jnp.sum(t * t, axis=-1, keepdims=True) * (1.0 / width) + EPS)


def _proj_body(x_ref, pos_ref, g_in_ref, w_in_ref, gq_ref, w_uq_ref, gkv_ref, w_ukv_ref,
               qhg_ref, khg_ref, vgg_ref, ind_ref, wpair_ref, bpair_ref, gob_ref,
               invf_ref, sgn_ref, vone_ref,
               q_out, k_out, v_out, ga_out, mb_out, w_in_scr):
    @pl.when(pl.program_id(0) == 0)
    def _():
        for lo in range(0, D_IN_EXT, W_T_BLOCK):
            w_in_scr[:, lo:lo + W_T_BLOCK] = w_in_ref[lo:lo + W_T_BLOCK, :].T

    x = x_ref[...]
    h = (x * _rms(x, D_MODEL) * g_in_ref[...]).astype(bf16)

    def proj(lo, width):
        return jnp.dot(h, w_in_scr[:, lo:lo + width], preferred_element_type=f32)

    lane = lax.broadcasted_iota(jnp.int32, (1, LANES), 1)
    rope_lanes = (lane >= QK_NOPE_DIM) & (lane < QK_DIM)
    ang = pos_ref[...].astype(f32) * invf_ref[...]
    cos_c, sin_c = jnp.cos(ang), jnp.sin(ang)
    cos_rows, sin_rows = [], []
    for g in range(LANES // QK_ROPE_DIM):
        shift = (QK_NOPE_DIM - QK_ROPE_DIM * g) % LANES
        cos_rows.append(pltpu.roll(cos_c, shift, 1) if shift else cos_c)
        sin_rows.append(pltpu.roll(sin_c, shift, 1) if shift else sin_c)
    cos = jnp.where(rope_lanes, jnp.concatenate(cos_rows, axis=0), 1.0)
    first = (lane >= QK_NOPE_DIM) & (lane < QK_NOPE_DIM + HALF_ROPE)
    sin_signed = jnp.concatenate(sin_rows, axis=0) * sgn_ref[...]

    def rope(t):
        swapped = jnp.where(first, pltpu.roll(t, LANES - HALF_ROPE, 1), pltpu.roll(t, HALF_ROPE, 1))
        return t * cos + swapped * sin_signed

    c_q = proj(OFF_CQ, Q_LORA_RANK)
    cqn = (c_q * _rms(c_q, Q_LORA_RANK) * gq_ref[...]).astype(bf16)
    qhg = qhg_ref[...]
    for pair in range(A_HEADS // 2):
        qq = jnp.dot(cqn, w_uq_ref[:, 2 * pair * LANES:2 * (pair + 1) * LANES], preferred_element_type=f32)
        for sub in range(2):
            qh = rope(qq[:, sub * LANES:(sub + 1) * LANES])
            q_out[0, 2 * pair + sub] = (qh * _rms(qh, QK_DIM) * qhg).astype(bf16)

    c_kv = proj(OFF_CKV, KV_LORA_RANK)
    ckvn = (c_kv * _rms(c_kv, KV_LORA_RANK) * gkv_ref[...]).astype(bf16)
    kr = proj(OFF_KR, LANES)
    kpe = jnp.where(rope_lanes, rope(kr), 0.0)
    khg = khg_ref[...]
    for hd in range(A_HEADS):
        kv = jnp.dot(ckvn, w_ukv_ref[:, 2 * hd * LANES:2 * (hd + 1) * LANES], preferred_element_type=f32)
        kh = kv[:, :LANES] + kpe
        k_out[0, hd] = (kh * _rms(kh, QK_DIM) * khg).astype(bf16)
        vh = kv[:, LANES:] + vone_ref[...]
        v_out[0, hd] = vh.T[0:V_ROWS].astype(bf16)

    ga_out[...] = _silu(proj(OFF_ZA, A_WIDTH)).astype(bf16)

    u = _gelu(proj(OFF_U, B_WIDTH))
    v = _gelu(proj(OFF_V, B_WIDTH))
    vv = v * v
    vv_hi = vv.astype(bf16)
    vv_lo = (vv - vv_hi.astype(f32)).astype(bf16)
    ind = ind_ref[...]
    ssq = (jnp.dot(vv_hi, ind, preferred_element_type=f32) + jnp.dot(vv_lo, ind, preferred_element_type=f32))
    vn = (v * lax.rsqrt(ssq * (1.0 / B_HEAD_DIM) + EPS) * vgg_ref[...]).astype(bf16)
    lower = lane < B_HEAD_DIM
    zero = jnp.zeros((), bf16)
    chunks = []
    for c in range(TOK_TILE // CHUNK):
        pairs = []
        for p in range(B_HEADS // 2):
            vp = vn[c * CHUNK:(c + 1) * CHUNK, p * LANES:(p + 1) * LANES]
            rhs = jnp.concatenate([jnp.where(lower, vp, zero), jnp.where(lower, zero, vp)], axis=0)
            sv = jnp.dot(wpair_ref[p], rhs, preferred_element_type=f32) + bpair_ref[p]
            pairs.append(u[c * CHUNK:(c + 1) * CHUNK, p * LANES:(p + 1) * LANES] * sv)
        chunks.append(jnp.concatenate(pairs, axis=1))
    ob = jnp.concatenate(chunks, axis=0)
    obn = ob * _rms(ob, B_WIDTH) * gob_ref[...]
    mb_out[...] = (obn * _silu(proj(OFF_ZB, B_WIDTH))).astype(bf16)


def _attn_body(q_ref, k_ref, v_ref, ga_ref, mb_ref, x_ref, goa_ref, w_out_ref, y_ref,
               s_even, s_odd, acc_scr):
    seq = k_ref.shape[2]
    n_kv = seq // KV_TILE

    def head_step(hd, parity, m_prev, do_scores=True, do_weighted=True):
        s_cur, s_prev = (s_even, s_odd) if parity == 0 else (s_odd, s_even)
        q_t = q_ref[0, hd].T if do_scores else None
        m = None
        acc = [None, None]
        for t in range(n_kv):
            keys = slice(t * KV_TILE, (t + 1) * KV_TILE)
            if do_scores:
                s = jnp.dot(k_ref[0, hd, keys, :], q_t, preferred_element_type=f32)
                s_cur[keys, :] = s
                m = s if m is None else jnp.maximum(m, s)
            if do_weighted:
                p = jnp.exp2(s_prev[keys, :] - m_prev).astype(bf16)
                pv = jnp.dot(v_ref[0, hd - 1, :, keys], p, preferred_element_type=f32)
                acc[t % 2] = pv if acc[t % 2] is None else acc[t % 2] + pv
        if do_weighted:
            acc_scr[hd - 1] = acc[0] + acc[1]
        return jnp.max(m, axis=0, keepdims=True) if do_scores else None

    def pair_step(i, m_prev):
        m_odd = head_step(2 * i + 1, 1, m_prev)
        return head_step(2 * i + 2, 0, m_odd)

    m_cur = head_step(0, 0, None, do_weighted=False)
    m_cur = lax.fori_loop(0, (A_HEADS - 2) // 2, pair_step, m_cur)
    m_cur = head_step(A_HEADS - 1, 1, m_cur)
    half = D_MODEL // 2
    y_b = [x_ref[:, c:c + half] + jnp.dot(mb_ref[...], w_out_ref[A_WIDTH:, c:c + half], preferred_element_type=f32)
           for c in (0, half)]
    head_step(A_HEADS, 0, m_cur, do_scores=False)

    heads = []
    for hd in range(A_HEADS):
        acc = acc_scr[hd]
        heads.append(acc[0:A_V_DIM] / acc[A_V_DIM:A_V_DIM + 1])
    o = jnp.concatenate(heads, axis=0).T
    on = o * lax.rsqrt(jnp.mean(o * o, axis=-1, keepdims=True) + EPS) * goa_ref[...]
    mix_a = (on * ga_ref[...].astype(f32)).astype(bf16)
    for i, c in enumerate((0, half)):
        y_ref[:, c:c + half] = y_b[i] + jnp.dot(mix_a, w_out_ref[0:A_WIDTH, c:c + half], preferred_element_type=f32)


def _pad_heads(w, per_head, left=0):
    rows = w.shape[0]
    w = w.reshape(rows, A_HEADS, per_head)
    w = jnp.pad(w, ((0, 0), (0, 0), (left, LANES - per_head - left)))
    return w.reshape(rows, HEAD_W)


def _interleave_heads(a, b):
    rows = a.shape[0]
    return jnp.stack([a.reshape(rows, A_HEADS, LANES), b.reshape(rows, A_HEADS, LANES)], axis=2).reshape(rows, 2 * HEAD_W)


def _const_spec(shape):
    return pl.BlockSpec(shape, lambda *_: (0,) * len(shape))


def kernel(x, positions, norm_in_g, w_in, q_lora_g, w_uq, kv_lora_g, w_ukv, q_head_g, k_head_g,
           v_gate_g, w_s, b_s, out_a_g, out_b_g, w_out):
    batch, seq, _ = x.shape
    n_tok = batch * seq
    assert seq % Q_TILE == 0 and seq % KV_TILE == 0 and seq % TOK_TILE == 0 and TOK_TILE % CHUNK == 0

    w_t = w_in.T
    w_in_ext = jnp.concatenate(
        [w_t[:OFF_KR], jnp.zeros((QK_NOPE_DIM, D_MODEL), f32), w_t[OFF_KR:OFF_KR + QK_ROPE_DIM],
         jnp.zeros((LANES - QK_DIM, D_MODEL), f32), w_t[OFF_KR + QK_ROPE_DIM:]], axis=0).astype(bf16)

    w_uq_ext = _pad_heads(w_uq, QK_DIM).astype(bf16)

    ukv = w_ukv.reshape(KV_LORA_RANK, A_HEADS, QK_NOPE_DIM + A_V_DIM)
    uk = _pad_heads(ukv[..., :QK_NOPE_DIM].reshape(KV_LORA_RANK, -1), QK_NOPE_DIM)
    uv = _pad_heads(ukv[..., QK_NOPE_DIM:].reshape(KV_LORA_RANK, -1), A_V_DIM)
    w_ukv_ext = _interleave_heads(uk, uv).astype(bf16)
    vone = np.zeros((1, LANES), np.float32)
    vone[0, A_V_DIM] = 1.0
    vone = jnp.asarray(vone)

    scale = math.log2(math.e) / math.sqrt(QK_DIM)
    qhg = jnp.pad(q_head_g * scale, (0, LANES - QK_DIM)).reshape(1, LANES)
    khg = jnp.pad(k_head_g, (0, LANES - QK_DIM)).reshape(1, LANES)

    inv_freq = 1.0 / (ROPE_THETA ** (jnp.arange(0, QK_ROPE_DIM, 2, dtype=f32) / QK_ROPE_DIM))
    invf = jnp.tile(inv_freq, LANES // HALF_ROPE).reshape(1, LANES)
    sgn = np.zeros((1, LANES), np.float32)
    sgn[0, QK_NOPE_DIM:QK_NOPE_DIM + HALF_ROPE] = -1.0
    sgn[0, QK_NOPE_DIM + HALF_ROPE:QK_DIM] = 1.0
    sgn = jnp.asarray(sgn)

    head_of = np.arange(B_WIDTH) // B_HEAD_DIM
    ind = jnp.asarray((head_of[:, None] == head_of[None, :]).astype(np.float32)).astype(bf16)
    wpair = jnp.concatenate([w_s[0::2], w_s[1::2]], axis=2).astype(bf16)
    bpair = jnp.concatenate([jnp.broadcast_to(b_s[0::2, :, None], (B_HEADS // 2, CHUNK, B_HEAD_DIM)),
                             jnp.broadcast_to(b_s[1::2, :, None], (B_HEADS // 2, CHUNK, B_HEAD_DIM))], axis=2)

    x2 = x.reshape(n_tok, D_MODEL)
    groups = LANES // QK_ROPE_DIM
    pos2 = positions.reshape(n_tok // TOK_TILE, groups, ROPE_ROWS).transpose(0, 2, 1)
    pos2 = jnp.repeat(pos2, QK_ROPE_DIM, axis=2).reshape(n_tok // groups, LANES)
    tiles_per_row = seq // TOK_TILE

    head_out = lambda t: pl.BlockSpec((1, A_HEADS, t, LANES), lambda i: (i // tiles_per_row, 0, i % tiles_per_row, 0))
    q, k, v, ga, mb = pl.pallas_call(
        _proj_body,
        grid=(n_tok // TOK_TILE,),
        in_specs=[
            pl.BlockSpec((TOK_TILE, D_MODEL), lambda i: (i, 0)),
            pl.BlockSpec((ROPE_ROWS, LANES), lambda i: (i, 0)),
            _const_spec((1, D_MODEL)),
            _const_spec((D_IN_EXT, D_MODEL)),
            _const_spec((1, Q_LORA_RANK)),
            _const_spec((Q_LORA_RANK, HEAD_W)),
            _const_spec((1, KV_LORA_RANK)),
            _const_spec((KV_LORA_RANK, 2 * HEAD_W)),
            _const_spec((1, LANES)),
            _const_spec((1, LANES)),
            _const_spec((1, B_WIDTH)),
            _const_spec((B_WIDTH, B_WIDTH)),
            _const_spec((B_HEADS // 2, CHUNK, 2 * CHUNK)),
            _const_spec((B_HEADS // 2, CHUNK, LANES)),
            _const_spec((1, B_WIDTH)),
            _const_spec((1, LANES)),
            _const_spec((1, LANES)),
            _const_spec((1, LANES)),
        ],
        out_specs=[
            head_out(TOK_TILE), head_out(TOK_TILE),
            pl.BlockSpec((1, A_HEADS, V_ROWS, TOK_TILE), lambda i: (i // tiles_per_row, 0, 0, i % tiles_per_row)),
            pl.BlockSpec((TOK_TILE, A_WIDTH), lambda i: (i, 0)),
            pl.BlockSpec((TOK_TILE, B_WIDTH), lambda i: (i, 0)),
        ],
        out_shape=[
            jax.ShapeDtypeStruct((batch, A_HEADS, seq, LANES), bf16),
            jax.ShapeDtypeStruct((batch, A_HEADS, seq, LANES), bf16),
            jax.ShapeDtypeStruct((batch, A_HEADS, V_ROWS, seq), bf16),
            jax.ShapeDtypeStruct((n_tok, A_WIDTH), bf16),
            jax.ShapeDtypeStruct((n_tok, B_WIDTH), bf16),
        ],
        scratch_shapes=[pltpu.VMEM((D_MODEL, D_IN_EXT), bf16)],
        compiler_params=pltpu.CompilerParams(dimension_semantics=("arbitrary",), vmem_limit_bytes=VMEM_LIMIT),
        name="proj",
    )(x2, pos2, norm_in_g.reshape(1, D_MODEL), w_in_ext, q_lora_g.reshape(1, -1), w_uq_ext,
      kv_lora_g.reshape(1, -1), w_ukv_ext, qhg, khg, v_gate_g.reshape(1, B_WIDTH), ind, wpair, bpair,
      out_b_g.reshape(1, B_WIDTH), invf, sgn, vone)

    q_tiles = seq // Q_TILE
    tok_blk = lambda w: pl.BlockSpec((Q_TILE, w), lambda b, i: (b * q_tiles + i, 0))
    resident = pl.Buffered(1)
    y = pl.pallas_call(
        _attn_body,
        grid=(batch, q_tiles),
        in_specs=[
            pl.BlockSpec((1, A_HEADS, Q_TILE, LANES), lambda b, i: (b, 0, i, 0)),
            pl.BlockSpec((1, A_HEADS, seq, LANES), lambda b, i: (b, 0, 0, 0), pipeline_mode=resident),
            pl.BlockSpec((1, A_HEADS, V_ROWS, seq), lambda b, i: (b, 0, 0, 0), pipeline_mode=resident),
            tok_blk(A_WIDTH), tok_blk(B_WIDTH), tok_blk(D_MODEL),
            pl.BlockSpec((1, A_WIDTH), lambda b, i: (0, 0), pipeline_mode=resident),
            pl.BlockSpec((D_MODEL, D_MODEL), lambda b, i: (0, 0), pipeline_mode=resident),
        ],
        out_specs=tok_blk(D_MODEL),
        out_shape=jax.ShapeDtypeStruct((n_tok, D_MODEL), f32),
        scratch_shapes=[pltpu.VMEM((seq, Q_TILE), f32), pltpu.VMEM((seq, Q_TILE), f32),
                        pltpu.VMEM((A_HEADS, V_ROWS, Q_TILE), f32)],
        compiler_params=pltpu.CompilerParams(dimension_semantics=("arbitrary", "arbitrary"),
                                             vmem_limit_bytes=VMEM_LIMIT),
        name="attn",
    )(q, k, v, ga, mb, x2, out_a_g.reshape(1, A_WIDTH), w_out.astype(bf16))
    return y.reshape(batch, seq, D_MODEL)
```

```python
import math

import jax
import jax.numpy as jnp
import numpy as np
from jax import lax
from jax.experimental import pallas as pl
from jax.experimental.pallas import tpu as pltpu

D_MODEL = 1024
A_HEADS = 8
A_V_DIM = 64
A_WIDTH = A_HEADS * A_V_DIM
QK_NOPE_DIM = 64
QK_ROPE_DIM = 32
HALF_ROPE = QK_ROPE_DIM // 2
QK_DIM = QK_NOPE_DIM + QK_ROPE_DIM
Q_LORA_RANK = 256
KV_LORA_RANK = 128
ROPE_THETA = 10000.0
B_HEADS = 8
B_HEAD_DIM = 64
B_WIDTH = B_HEADS * B_HEAD_DIM
CHUNK = 128
EPS = 1e-6

LANES = 128
SUBLANES = 8
HEAD_W = A_HEADS * LANES
V_ROWS = 80

OFF_CQ = 0
OFF_CKV = OFF_CQ + Q_LORA_RANK
OFF_KR = OFF_CKV + KV_LORA_RANK
OFF_ZA = OFF_KR + LANES
OFF_U = OFF_ZA + A_WIDTH
OFF_V = OFF_U + B_WIDTH
OFF_ZB = OFF_V + B_WIDTH
D_IN_EXT = OFF_ZB + B_WIDTH

TOK_TILE = 512
W_T_BLOCK = 256
SUB_TILE = 512
ROPE_ROWS = SUB_TILE * QK_ROPE_DIM // LANES
Q_TILE = 512
KV_TILE = 1024
VMEM_LIMIT = 63 * 1024 * 1024

f32 = jnp.float32
bf16 = jnp.bfloat16


def _gelu(t):
    return jax.nn.gelu(t)


def _silu(t):
    half = 0.5 * t
    return half + half * jnp.tanh(half)


def _rms(t, width):
    return lax.rsqrt(jnp.sum(t * t, axis=-1, keepdims=True) * (1.0 / width) + EPS)


def _proj_body(x_ref, pos_ref, g_in_ref, w_in_ref, gq_ref, w_uq_ref, gkv_ref, w_ukv_ref,
               qhg_ref, khg_ref, vgg_ref, ind_ref, wpair_ref, bpair_ref, gob_ref,
               invf_ref, sgn_ref, vone_ref,
               q_out, k_out, v_out, ga_out, mb_out, w_in_scr):
    @pl.when(pl.program_id(0) == 0)
    def _():
        for lo in range(0, D_IN_EXT, W_T_BLOCK):
            w_in_scr[:, lo:lo + W_T_BLOCK] = w_in_ref[lo:lo + W_T_BLOCK, :].T

    for sub in range(TOK_TILE // SUB_TILE):
        _proj_rows(sub, x_ref, pos_ref, g_in_ref, gq_ref, w_uq_ref, gkv_ref, w_ukv_ref,
                   qhg_ref, khg_ref, vgg_ref, ind_ref, wpair_ref, bpair_ref, gob_ref,
                   invf_ref, sgn_ref, vone_ref,
                   q_out, k_out, v_out, ga_out, mb_out, w_in_scr)


def _proj_rows(sub, x_ref, pos_ref, g_in_ref, gq_ref, w_uq_ref, gkv_ref, w_ukv_ref,
               qhg_ref, khg_ref, vgg_ref, ind_ref, wpair_ref, bpair_ref, gob_ref,
               invf_ref, sgn_ref, vone_ref,
               q_out, k_out, v_out, ga_out, mb_out, w_in_scr):
    rows = slice(sub * SUB_TILE, (sub + 1) * SUB_TILE)
    x = x_ref[rows, :]
    h = (x * _rms(x, D_MODEL) * g_in_ref[...]).astype(bf16)

    def proj(lo, width):
        return jnp.dot(h, w_in_scr[:, lo:lo + width], preferred_element_type=f32)

    lane = lax.broadcasted_iota(jnp.int32, (1, LANES), 1)
    rope_lanes = (lane >= QK_NOPE_DIM) & (lane < QK_DIM)
    ang = pos_ref[sub * ROPE_ROWS:(sub + 1) * ROPE_ROWS, :].astype(f32) * invf_ref[...]
    cos_c, sin_c = jnp.cos(ang), jnp.sin(ang)
    cos_rows, sin_rows = [], []
    for g in range(LANES // QK_ROPE_DIM):
        shift = (QK_NOPE_DIM - QK_ROPE_DIM * g) % LANES
        cos_rows.append(pltpu.roll(cos_c, shift, 1) if shift else cos_c)
        sin_rows.append(pltpu.roll(sin_c, shift, 1) if shift else sin_c)
    cos = jnp.where(rope_lanes, jnp.concatenate(cos_rows, axis=0), 1.0)
    first = (lane >= QK_NOPE_DIM) & (lane < QK_NOPE_DIM + HALF_ROPE)
    sin_signed = jnp.concatenate(sin_rows, axis=0) * sgn_ref[...]

    def rope(t):
        swapped = jnp.where(first, pltpu.roll(t, LANES - HALF_ROPE, 1), pltpu.roll(t, HALF_ROPE, 1))
        return t * cos + swapped * sin_signed

    c_q = proj(OFF_CQ, Q_LORA_RANK)
    cqn = (c_q * _rms(c_q, Q_LORA_RANK) * gq_ref[...]).astype(bf16)
    qhg = qhg_ref[...]
    for pair in range(A_HEADS // 2):
        qq = jnp.dot(cqn, w_uq_ref[:, 2 * pair * LANES:2 * (pair + 1) * LANES], preferred_element_type=f32)
        for side in range(2):
            qh = rope(qq[:, side * LANES:(side + 1) * LANES])
            q_out[0, 2 * pair + side, rows, :] = (qh * _rms(qh, QK_DIM) * qhg).astype(bf16)

    c_kv = proj(OFF_CKV, KV_LORA_RANK)
    ckvn = (c_kv * _rms(c_kv, KV_LORA_RANK) * gkv_ref[...]).astype(bf16)
    kr = proj(OFF_KR, LANES)
    kpe = jnp.where(rope_lanes, rope(kr), 0.0)
    khg = khg_ref[...]
    for hd in range(A_HEADS):
        kv = jnp.dot(ckvn, w_ukv_ref[:, 2 * hd * LANES:2 * (hd + 1) * LANES], preferred_element_type=f32)
        kh = kv[:, :LANES] + kpe
        k_out[0, hd, rows, :] = (kh * _rms(kh, QK_DIM) * khg).astype(bf16)
        vh = kv[:, LANES:] + vone_ref[...]
        v_out[0, hd, :, rows] = vh.T[0:V_ROWS].astype(bf16)

    ga_out[rows, :] = _silu(proj(OFF_ZA, A_WIDTH)).astype(bf16)

    u = _gelu(proj(OFF_U, B_WIDTH))
    v = _gelu(proj(OFF_V, B_WIDTH))
    vv = v * v
    vv_hi = vv.astype(bf16)
    vv_lo = (vv - vv_hi.astype(f32)).astype(bf16)
    ind = ind_ref[...]
    ssq = (jnp.dot(vv_hi, ind, preferred_element_type=f32) + jnp.dot(vv_lo, ind, preferred_element_type=f32))
    vn = (v * lax.rsqrt(ssq * (1.0 / B_HEAD_DIM) + EPS) * vgg_ref[...]).astype(bf16)
    lower = lane < B_HEAD_DIM
    zero = jnp.zeros((), bf16)
    chunks = []
    for c in range(SUB_TILE // CHUNK):
        pairs = []
        for p in range(B_HEADS // 2):
            vp = vn[c * CHUNK:(c + 1) * CHUNK, p * LANES:(p + 1) * LANES]
            rhs = jnp.concatenate([jnp.where(lower, vp, zero), jnp.where(lower, zero, vp)], axis=0)
            sv = jnp.dot(wpair_ref[p], rhs, preferred_element_type=f32) + bpair_ref[p]
            pairs.append(u[c * CHUNK:(c + 1) * CHUNK, p * LANES:(p + 1) * LANES] * sv)
        chunks.append(jnp.concatenate(pairs, axis=1))
    ob = jnp.concatenate(chunks, axis=0)
    obn = ob * _rms(ob, B_WIDTH) * gob_ref[...]
    mb_out[rows, :] = (obn * _silu(proj(OFF_ZB, B_WIDTH))).astype(bf16)


def _attn_body(q_ref, k_ref, v_ref, ga_ref, mb_ref, x_ref, goa_ref, w_out_ref, y_ref,
               s_even, s_odd, acc_scr):
    seq = k_ref.shape[2]
    n_kv = seq // KV_TILE

    def head_step(hd, parity, m_prev, do_scores=True, do_weighted=True):
        s_cur, s_prev = (s_even, s_odd) if parity == 0 else (s_odd, s_even)
        q_t = q_ref[0, hd].T if do_scores else None
        m = None
        acc = [None, None]
        for t in range(n_kv):
            keys = slice(t * KV_TILE, (t + 1) * KV_TILE)
            if do_scores:
                s = jnp.dot(k_ref[0, hd, keys, :], q_t, preferred_element_type=f32)
                s_cur[keys, :] = s
                s_max = jnp.max(s.reshape(KV_TILE // SUBLANES, SUBLANES, s.shape[1]), axis=0)
                m = s_max if m is None else jnp.maximum(m, s_max)
            if do_weighted:
                p = jnp.exp2(s_prev[keys, :] - m_prev).astype(bf16)
                pv = jnp.dot(v_ref[0, hd - 1, :, keys], p, preferred_element_type=f32)
                acc[t % 2] = pv if acc[t % 2] is None else acc[t % 2] + pv
        if do_weighted:
            acc_scr[hd - 1] = acc[0] + acc[1]
        return jnp.max(m, axis=0, keepdims=True) if do_scores else None

    def pair_step(i, m_prev):
        m_odd = head_step(2 * i + 1, 1, m_prev)
        return head_step(2 * i + 2, 0, m_odd)

    m_cur = head_step(0, 0, None, do_weighted=False)
    m_cur = lax.fori_loop(0, (A_HEADS - 2) // 2, pair_step, m_cur)
    m_cur = head_step(A_HEADS - 1, 1, m_cur)
    half = D_MODEL // 2
    y_b = [x_ref[:, c:c + half] + jnp.dot(mb_ref[...], w_out_ref[A_WIDTH:, c:c + half], preferred_element_type=f32)
           for c in (0, half)]
    head_step(A_HEADS, 0, m_cur, do_scores=False)

    heads = []
    for hd in range(A_HEADS):
        acc = acc_scr[hd]
        heads.append(acc[0:A_V_DIM] / acc[A_V_DIM:A_V_DIM + 1])
    o = jnp.concatenate(heads, axis=0).T
    on = o * lax.rsqrt(jnp.mean(o * o, axis=-1, keepdims=True) + EPS) * goa_ref[...]
    mix_a = (on * ga_ref[...].astype(f32)).astype(bf16)
    for i, c in enumerate((0, half)):
        y_ref[:, c:c + half] = y_b[i] + jnp.dot(mix_a, w_out_ref[0:A_WIDTH, c:c + half], preferred_element_type=f32)


def _pad_heads(w, per_head, left=0):
    rows = w.shape[0]
    w = w.reshape(rows, A_HEADS, per_head)
    w = jnp.pad(w, ((0, 0), (0, 0), (left, LANES - per_head - left)))
    return w.reshape(rows, HEAD_W)


def _interleave_heads(a, b):
    rows = a.shape[0]
    return jnp.stack([a.reshape(rows, A_HEADS, LANES), b.reshape(rows, A_HEADS, LANES)], axis=2).reshape(rows, 2 * HEAD_W)


def _const_spec(shape):
    return pl.BlockSpec(shape, lambda *_: (0,) * len(shape))


def kernel(x, positions, norm_in_g, w_in, q_lora_g, w_uq, kv_lora_g, w_ukv, q_head_g, k_head_g,
           v_gate_g, w_s, b_s, out_a_g, out_b_g, w_out):
    batch, seq, _ = x.shape
    n_tok = batch * seq
    assert seq % Q_TILE == 0 and seq % KV_TILE == 0 and seq % TOK_TILE == 0
    assert TOK_TILE % SUB_TILE == 0 and SUB_TILE % CHUNK == 0

    w_t = w_in.T
    w_in_ext = jnp.concatenate(
        [w_t[:OFF_KR], jnp.zeros((QK_NOPE_DIM, D_MODEL), f32), w_t[OFF_KR:OFF_KR + QK_ROPE_DIM],
         jnp.zeros((LANES - QK_DIM, D_MODEL), f32), w_t[OFF_KR + QK_ROPE_DIM:]], axis=0).astype(bf16)

    w_uq_ext = _pad_heads(w_uq, QK_DIM).astype(bf16)

    ukv = w_ukv.reshape(KV_LORA_RANK, A_HEADS, QK_NOPE_DIM + A_V_DIM)
    uk = _pad_heads(ukv[..., :QK_NOPE_DIM].reshape(KV_LORA_RANK, -1), QK_NOPE_DIM)
    uv = _pad_heads(ukv[..., QK_NOPE_DIM:].reshape(KV_LORA_RANK, -1), A_V_DIM)
    w_ukv_ext = _interleave_heads(uk, uv).astype(bf16)
    vone = np.zeros((1, LANES), np.float32)
    vone[0, A_V_DIM] = 1.0
    vone = jnp.asarray(vone)

    scale = math.log2(math.e) / math.sqrt(QK_DIM)
    qhg = jnp.pad(q_head_g * scale, (0, LANES - QK_DIM)).reshape(1, LANES)
    khg = jnp.pad(k_head_g, (0, LANES - QK_DIM)).reshape(1, LANES)

    inv_freq = 1.0 / (ROPE_THETA ** (jnp.arange(0, QK_ROPE_DIM, 2, dtype=f32) / QK_ROPE_DIM))
    invf = jnp.tile(inv_freq, LANES // HALF_ROPE).reshape(1, LANES)
    sgn = np.zeros((1, LANES), np.float32)
    sgn[0, QK_NOPE_DIM:QK_NOPE_DIM + HALF_ROPE] = -1.0
    sgn[0, QK_NOPE_DIM + HALF_ROPE:QK_DIM] = 1.0
    sgn = jnp.asarray(sgn)

    head_of = np.arange(B_WIDTH) // B_HEAD_DIM
    ind = jnp.asarray((head_of[:, None] == head_of[None, :]).astype(np.float32)).astype(bf16)
    wpair = jnp.concatenate([w_s[0::2], w_s[1::2]], axis=2).astype(bf16)
    bpair = jnp.concatenate([jnp.broadcast_to(b_s[0::2, :, None], (B_HEADS // 2, CHUNK, B_HEAD_DIM)),
                             jnp.broadcast_to(b_s[1::2, :, None], (B_HEADS // 2, CHUNK, B_HEAD_DIM))], axis=2)

    x2 = x.reshape(n_tok, D_MODEL)
    groups = LANES // QK_ROPE_DIM
    pos2 = positions.reshape(n_tok // SUB_TILE, groups, ROPE_ROWS).transpose(0, 2, 1)
    pos2 = jnp.repeat(pos2, QK_ROPE_DIM, axis=2).reshape(n_tok // groups, LANES)
    tiles_per_row = seq // TOK_TILE

    head_out = lambda t: pl.BlockSpec((1, A_HEADS, t, LANES), lambda i: (i // tiles_per_row, 0, i % tiles_per_row, 0))
    q, k, v, ga, mb = pl.pallas_call(
        _proj_body,
        grid=(n_tok // TOK_TILE,),
        in_specs=[
            pl.BlockSpec((TOK_TILE, D_MODEL), lambda i: (i, 0)),
            pl.BlockSpec((TOK_TILE // SUB_TILE * ROPE_ROWS, LANES), lambda i: (i, 0)),
            _const_spec((1, D_MODEL)),
            _const_spec((D_IN_EXT, D_MODEL)),
            _const_spec((1, Q_LORA_RANK)),
            _const_spec((Q_LORA_RANK, HEAD_W)),
            _const_spec((1, KV_LORA_RANK)),
            _const_spec((KV_LORA_RANK, 2 * HEAD_W)),
            _const_spec((1, LANES)),
            _const_spec((1, LANES)),
            _const_spec((1, B_WIDTH)),
            _const_spec((B_WIDTH, B_WIDTH)),
            _const_spec((B_HEADS // 2, CHUNK, 2 * CHUNK)),
            _const_spec((B_HEADS // 2, CHUNK, LANES)),
            _const_spec((1, B_WIDTH)),
            _const_spec((1, LANES)),
            _const_spec((1, LANES)),
            _const_spec((1, LANES)),
        ],
        out_specs=[
            head_out(TOK_TILE), head_out(TOK_TILE),
            pl.BlockSpec((1, A_HEADS, V_ROWS, TOK_TILE), lambda i: (i // tiles_per_row, 0, 0, i % tiles_per_row)),
            pl.BlockSpec((TOK_TILE, A_WIDTH), lambda i: (i, 0)),
            pl.BlockSpec((TOK_TILE, B_WIDTH), lambda i: (i, 0)),
        ],
        out_shape=[
            jax.ShapeDtypeStruct((batch, A_HEADS, seq, LANES), bf16),
            jax.ShapeDtypeStruct((batch, A_HEADS, seq, LANES), bf16),
            jax.ShapeDtypeStruct((batch, A_HEADS, V_ROWS, seq), bf16),
            jax.ShapeDtypeStruct((n_tok, A_WIDTH), bf16),
            jax.ShapeDtypeStruct((n_tok, B_WIDTH), bf16),
        ],
        scratch_shapes=[pltpu.VMEM((D_MODEL, D_IN_EXT), bf16)],
        compiler_params=pltpu.CompilerParams(dimension_semantics=("arbitrary",), vmem_limit_bytes=VMEM_LIMIT),
        name="proj",
    )(x2, pos2, norm_in_g.reshape(1, D_MODEL), w_in_ext, q_lora_g.reshape(1, -1), w_uq_ext,
      kv_lora_g.reshape(1, -1), w_ukv_ext, qhg, khg, v_gate_g.reshape(1, B_WIDTH), ind, wpair, bpair,
      out_b_g.reshape(1, B_WIDTH), invf, sgn, vone)

    q_tiles = seq // Q_TILE
    tok_blk = lambda w: pl.BlockSpec((Q_TILE, w), lambda b, i: (b * q_tiles + i, 0))
    resident = pl.Buffered(1)
    y = pl.pallas_call(
        _attn_body,
        grid=(batch, q_tiles),
        in_specs=[
            pl.BlockSpec((1, A_HEADS, Q_TILE, LANES), lambda b, i: (b, 0, i, 0)),
            pl.BlockSpec((1, A_HEADS, seq, LANES), lambda b, i: (b, 0, 0, 0)),
            pl.BlockSpec((1, A_HEADS, V_ROWS, seq), lambda b, i: (b, 0, 0, 0)),
            tok_blk(A_WIDTH), tok_blk(B_WIDTH), tok_blk(D_MODEL),
            pl.BlockSpec((1, A_WIDTH), lambda b, i: (0, 0), pipeline_mode=resident),
            pl.BlockSpec((D_MODEL, D_MODEL), lambda b, i: (0, 0), pipeline_mode=resident),
        ],
        out_specs=tok_blk(D_MODEL),
        out_shape=jax.ShapeDtypeStruct((n_tok, D_MODEL), f32),
        scratch_shapes=[pltpu.VMEM((seq, Q_TILE), f32), pltpu.VMEM((seq, Q_TILE), f32),
                        pltpu.VMEM((A_HEADS, V_ROWS, Q_TILE), f32)],
        compiler_params=pltpu.CompilerParams(dimension_semantics=("arbitrary", "arbitrary"),
                                             vmem_limit_bytes=VMEM_LIMIT),
        name="attn",
    )(q, k, v, ga, mb, x2, out_a_g.reshape(1, A_WIDTH), w_out.astype(bf16))
    return y.reshape(batch, seq, D_MODEL)
```

```python
import math

import jax
import jax.numpy as jnp
import numpy as np
from jax import lax
from jax.experimental import pallas as pl
from jax.experimental.pallas import tpu as pltpu

D_MODEL = 1024
A_HEADS = 8
A_V_DIM = 64
A_WIDTH = A_HEADS * A_V_DIM
QK_NOPE_DIM = 64
QK_ROPE_DIM = 32
HALF_ROPE = QK_ROPE_DIM // 2
QK_DIM = QK_NOPE_DIM + QK_ROPE_DIM
Q_LORA_RANK = 256
KV_LORA_RANK = 128
ROPE_THETA = 10000.0
B_HEADS = 8
B_HEAD_DIM = 64
B_WIDTH = B_HEADS * B_HEAD_DIM
CHUNK = 128
EPS = 1e-6

LANES = 128
SUBLANES = 8
HEAD_W = A_HEADS * LANES
V_ROWS = 80

OFF_CQ = 0
OFF_CKV = OFF_CQ + Q_LORA_RANK
OFF_KR = OFF_CKV + KV_LORA_RANK
OFF_ZA = OFF_KR + LANES
OFF_U = OFF_ZA + A_WIDTH
OFF_V = OFF_U + B_WIDTH
OFF_ZB = OFF_V + B_WIDTH
D_IN_EXT = OFF_ZB + B_WIDTH

TOK_TILE = 512
W_T_BLOCK = 256
ROPE_ROWS = TOK_TILE * QK_ROPE_DIM // LANES
Q_TILE = 512
KV_TILE = 1024
VMEM_LIMIT = 63 * 1024 * 1024

f32 = jnp.float32
bf16 = jnp.bfloat16


def _gelu(t):
    return jax.nn.gelu(t)


def _silu(t):
    half = 0.5 * t
    return half + half * jnp.tanh(half)


def _inv_norm(t, width):
    return lax.rsqrt(jnp.sum(t * t, axis=-1, keepdims=True) + width * EPS)


def _proj_body(x_ref, pos_ref, g_in_ref, w_in_ref, gq_ref, w_uq_ref, w_uq_swap_ref, gkv_ref, w_ukv_ref,
               qhg_ref, khg_ref, vgg_ref, ind_ref, wpair_ref, bpair_ref, gob_ref,
               invf_ref, sgn_ref,
               q_out, k_out, v_out, ga_out, mb_out, w_in_scr):
    @pl.when(pl.program_id(0) == 0)
    def _():
        for lo in range(0, D_IN_EXT, W_T_BLOCK):
            w_in_scr[:, lo:lo + W_T_BLOCK] = w_in_ref[lo:lo + W_T_BLOCK, :].T

    x = x_ref[...]
    h = (x * _inv_norm(x, D_MODEL) * (g_in_ref[...] * math.sqrt(D_MODEL))).astype(bf16)

    def proj(lo, width):
        return jnp.dot(h, w_in_scr[:, lo:lo + width], preferred_element_type=f32)

    lane = lax.broadcasted_iota(jnp.int32, (1, LANES), 1)
    rope_lanes = (lane >= QK_NOPE_DIM) & (lane < QK_DIM)
    ang = pos_ref[...].astype(f32) * invf_ref[...]
    cos_c, sin_c = jnp.cos(ang), jnp.sin(ang)
    cos_rows, sin_rows = [], []
    for g in range(LANES // QK_ROPE_DIM):
        shift = (QK_NOPE_DIM - QK_ROPE_DIM * g) % LANES
        cos_rows.append(pltpu.roll(cos_c, shift, 1) if shift else cos_c)
        sin_rows.append(pltpu.roll(sin_c, shift, 1) if shift else sin_c)
    cos = jnp.where(rope_lanes, jnp.concatenate(cos_rows, axis=0), 1.0)
    first = (lane >= QK_NOPE_DIM) & (lane < QK_NOPE_DIM + HALF_ROPE)
    sin_signed = jnp.concatenate(sin_rows, axis=0) * sgn_ref[...]

    def rope(t):
        swapped = jnp.where(first, pltpu.roll(t, LANES - HALF_ROPE, 1), pltpu.roll(t, HALF_ROPE, 1))
        return t * cos + swapped * sin_signed

    c_q = proj(OFF_CQ, Q_LORA_RANK)
    cqn = (c_q * _inv_norm(c_q, Q_LORA_RANK) * (gq_ref[...] * math.sqrt(Q_LORA_RANK))).astype(bf16)
    qhg = qhg_ref[...]
    for pair in range(A_HEADS // 2):
        cols = slice(2 * pair * LANES, 2 * (pair + 1) * LANES)
        qq = jnp.dot(cqn, w_uq_ref[:, cols], preferred_element_type=f32)
        qs = jnp.dot(cqn, w_uq_swap_ref[:, cols], preferred_element_type=f32)
        for side in range(2):
            lanes = slice(side * LANES, (side + 1) * LANES)
            qh = qq[:, lanes] * cos + qs[:, lanes] * sin_signed
            q_out[0, 2 * pair + side] = (qh * _inv_norm(qh, QK_DIM) * qhg).astype(bf16)

    c_kv = proj(OFF_CKV, KV_LORA_RANK)
    ckvn = (c_kv * _inv_norm(c_kv, KV_LORA_RANK) * (gkv_ref[...] * math.sqrt(KV_LORA_RANK))).astype(bf16)
    kr = proj(OFF_KR, LANES)
    kpe = jnp.where(rope_lanes, rope(kr), 0.0)
    khg = khg_ref[...]
    nope_lanes = lane < QK_NOPE_DIM
    tail_row = lax.broadcasted_iota(jnp.int32, (V_ROWS - A_V_DIM, TOK_TILE), 0)
    v_tail = jnp.where(tail_row == 0, 1.0, 0.0).astype(bf16)
    for pair in range(A_HEADS // 2):
        kv = jnp.dot(ckvn, w_ukv_ref[:, 2 * pair * LANES:2 * (pair + 1) * LANES], preferred_element_type=f32)
        for side in range(2):
            hd = 2 * pair + side
            group = kv[:, side * LANES:(side + 1) * LANES]
            kh = jnp.where(nope_lanes, group, kpe)
            k_out[0, hd] = (kh * _inv_norm(kh, QK_DIM) * khg).astype(bf16)
            v_out[0, hd, 0:A_V_DIM, :] = group.T[QK_NOPE_DIM:].astype(bf16)
            v_out[0, hd, A_V_DIM:, :] = v_tail

    ga_out[...] = _silu(proj(OFF_ZA, A_WIDTH)).astype(bf16)

    u = _gelu(proj(OFF_U, B_WIDTH))
    v = _gelu(proj(OFF_V, B_WIDTH))
    ssq = jnp.dot((v * v).astype(bf16), ind_ref[...], preferred_element_type=f32)
    vn = (v * lax.rsqrt(ssq + B_HEAD_DIM * EPS) * (vgg_ref[...] * math.sqrt(B_HEAD_DIM))).astype(bf16)
    lower = lane < B_HEAD_DIM
    zero = jnp.zeros((), bf16)
    chunks = []
    for c in range(TOK_TILE // CHUNK):
        pairs = []
        for p in range(B_HEADS // 2):
            vp = vn[c * CHUNK:(c + 1) * CHUNK, p * LANES:(p + 1) * LANES]
            rhs = jnp.concatenate([jnp.where(lower, vp, zero), jnp.where(lower, zero, vp)], axis=0)
            sv = jnp.dot(wpair_ref[p], rhs, preferred_element_type=f32) + bpair_ref[p]
            pairs.append(u[c * CHUNK:(c + 1) * CHUNK, p * LANES:(p + 1) * LANES] * sv)
        chunks.append(jnp.concatenate(pairs, axis=1))
    ob = jnp.concatenate(chunks, axis=0)
    obn = ob * _inv_norm(ob, B_WIDTH) * (gob_ref[...] * math.sqrt(B_WIDTH))
    mb_out[...] = (obn * _silu(proj(OFF_ZB, B_WIDTH))).astype(bf16)


def _attn_body(q_ref, k_ref, v_ref, ga_ref, mb_ref, x_ref, goa_ref, w_out_ref, y_ref,
               s_even, s_odd, acc_scr):
    seq = k_ref.shape[2]
    n_kv = seq // KV_TILE

    def head_step(hd, parity, m_prev, do_scores=True, do_weighted=True):
        s_cur, s_prev = (s_even, s_odd) if parity == 0 else (s_odd, s_even)
        q_t = q_ref[0, hd].T if do_scores else None
        m = None
        acc = [None, None]
        for t in range(n_kv):
            keys = slice(t * KV_TILE, (t + 1) * KV_TILE)
            if do_scores:
                s = jnp.dot(k_ref[0, hd, keys, :], q_t, preferred_element_type=f32)
                s_cur[keys, :] = s
                s_max = jnp.max(s.reshape(KV_TILE // SUBLANES, SUBLANES, s.shape[1]), axis=0)
                m = s_max if m is None else jnp.maximum(m, s_max)
            if do_weighted:
                p = jnp.exp2(s_prev[keys, :] - m_prev).astype(bf16)
                pv = jnp.dot(v_ref[0, hd - 1, :, keys], p, preferred_element_type=f32)
                acc[t % 2] = pv if acc[t % 2] is None else acc[t % 2] + pv
        if do_weighted:
            acc_scr[hd - 1] = acc[0] + acc[1]
        return jnp.max(m, axis=0, keepdims=True) if do_scores else None

    def pair_step(i, m_prev):
        m_odd = head_step(2 * i + 1, 1, m_prev)
        return head_step(2 * i + 2, 0, m_odd)

    m_cur = head_step(0, 0, None, do_weighted=False)
    m_cur = lax.fori_loop(0, (A_HEADS - 2) // 2, pair_step, m_cur)
    m_cur = head_step(A_HEADS - 1, 1, m_cur)
    half = D_MODEL // 2
    y_b = [x_ref[:, c:c + half] + jnp.dot(mb_ref[...], w_out_ref[A_WIDTH:, c:c + half], preferred_element_type=f32)
           for c in (0, half)]
    head_step(A_HEADS, 0, m_cur, do_scores=False)

    heads = []
    for hd in range(A_HEADS):
        acc = acc_scr[hd]
        heads.append(acc[0:A_V_DIM] / acc[A_V_DIM:A_V_DIM + 1])
    o = jnp.concatenate(heads, axis=0).T
    on = o * _inv_norm(o, A_WIDTH) * (goa_ref[...] * math.sqrt(A_WIDTH))
    mix_a = (on * ga_ref[...].astype(f32)).astype(bf16)
    for i, c in enumerate((0, half)):
        y_ref[:, c:c + half] = y_b[i] + jnp.dot(mix_a, w_out_ref[0:A_WIDTH, c:c + half], preferred_element_type=f32)


def _pad_heads(w, per_head, left=0):
    rows = w.shape[0]
    w = w.reshape(rows, A_HEADS, per_head)
    w = jnp.pad(w, ((0, 0), (0, 0), (left, LANES - per_head - left)))
    return w.reshape(rows, HEAD_W)


def _const_spec(shape):
    return pl.BlockSpec(shape, lambda *_: (0,) * len(shape))


def kernel(x, positions, norm_in_g, w_in, q_lora_g, w_uq, kv_lora_g, w_ukv, q_head_g, k_head_g,
           v_gate_g, w_s, b_s, out_a_g, out_b_g, w_out):
    batch, seq, _ = x.shape
    n_tok = batch * seq
    assert seq % Q_TILE == 0 and seq % KV_TILE == 0 and seq % TOK_TILE == 0 and TOK_TILE % CHUNK == 0

    w_t = w_in.T
    w_in_ext = jnp.concatenate(
        [w_t[:OFF_KR], jnp.zeros((QK_NOPE_DIM, D_MODEL), f32), w_t[OFF_KR:OFF_KR + QK_ROPE_DIM],
         jnp.zeros((LANES - QK_DIM, D_MODEL), f32), w_t[OFF_KR + QK_ROPE_DIM:]], axis=0).astype(bf16)

    w_uq_ext = _pad_heads(w_uq, QK_DIM).astype(bf16)
    uq = w_uq.reshape(Q_LORA_RANK, A_HEADS, QK_DIM)
    uq_swapped = jnp.concatenate(
        [jnp.zeros_like(uq[..., :QK_NOPE_DIM]), uq[..., QK_NOPE_DIM + HALF_ROPE:], uq[..., QK_NOPE_DIM:QK_NOPE_DIM + HALF_ROPE]],
        axis=-1)
    w_uq_swap = _pad_heads(uq_swapped.reshape(Q_LORA_RANK, -1), QK_DIM).astype(bf16)

    assert QK_NOPE_DIM + A_V_DIM == LANES

    qhg = jnp.pad(q_head_g * math.log2(math.e), (0, LANES - QK_DIM)).reshape(1, LANES)
    khg = jnp.pad(k_head_g * math.sqrt(QK_DIM), (0, LANES - QK_DIM)).reshape(1, LANES)

    inv_freq = 1.0 / (ROPE_THETA ** (jnp.arange(0, QK_ROPE_DIM, 2, dtype=f32) / QK_ROPE_DIM))
    invf = jnp.tile(inv_freq, LANES // HALF_ROPE).reshape(1, LANES)
    sgn = np.zeros((1, LANES), np.float32)
    sgn[0, QK_NOPE_DIM:QK_NOPE_DIM + HALF_ROPE] = -1.0
    sgn[0, QK_NOPE_DIM + HALF_ROPE:QK_DIM] = 1.0
    sgn = jnp.asarray(sgn)

    head_of = np.arange(B_WIDTH) // B_HEAD_DIM
    ind = jnp.asarray((head_of[:, None] == head_of[None, :]).astype(np.float32)).astype(bf16)
    wpair = jnp.concatenate([w_s[0::2], w_s[1::2]], axis=2).astype(bf16)
    bpair = jnp.concatenate([jnp.broadcast_to(b_s[0::2, :, None], (B_HEADS // 2, CHUNK, B_HEAD_DIM)),
                             jnp.broadcast_to(b_s[1::2, :, None], (B_HEADS // 2, CHUNK, B_HEAD_DIM))], axis=2)

    x2 = x.reshape(n_tok, D_MODEL)
    groups = LANES // QK_ROPE_DIM
    pos2 = positions.reshape(n_tok // TOK_TILE, groups, ROPE_ROWS).transpose(0, 2, 1)
    pos2 = jnp.repeat(pos2, QK_ROPE_DIM, axis=2).reshape(n_tok // groups, LANES)
    tiles_per_row = seq // TOK_TILE

    head_out = lambda t: pl.BlockSpec((1, A_HEADS, t, LANES), lambda i: (i // tiles_per_row, 0, i % tiles_per_row, 0))
    q, k, v, ga, mb = pl.pallas_call(
        _proj_body,
        grid=(n_tok // TOK_TILE,),
        in_specs=[
            pl.BlockSpec((TOK_TILE, D_MODEL), lambda i: (i, 0)),
            pl.BlockSpec((ROPE_ROWS, LANES), lambda i: (i, 0)),
            _const_spec((1, D_MODEL)),
            _const_spec((D_IN_EXT, D_MODEL)),
            _const_spec((1, Q_LORA_RANK)),
            _const_spec((Q_LORA_RANK, HEAD_W)),
            _const_spec((Q_LORA_RANK, HEAD_W)),
            _const_spec((1, KV_LORA_RANK)),
            _const_spec((KV_LORA_RANK, HEAD_W)),
            _const_spec((1, LANES)),
            _const_spec((1, LANES)),
            _const_spec((1, B_WIDTH)),
            _const_spec((B_WIDTH, B_WIDTH)),
            _const_spec((B_HEADS // 2, CHUNK, 2 * CHUNK)),
            _const_spec((B_HEADS // 2, CHUNK, LANES)),
            _const_spec((1, B_WIDTH)),
            _const_spec((1, LANES)),
            _const_spec((1, LANES)),
        ],
        out_specs=[
            head_out(TOK_TILE), head_out(TOK_TILE),
            pl.BlockSpec((1, A_HEADS, V_ROWS, TOK_TILE), lambda i: (i // tiles_per_row, 0, 0, i % tiles_per_row)),
            pl.BlockSpec((TOK_TILE, A_WIDTH), lambda i: (i, 0)),
            pl.BlockSpec((TOK_TILE, B_WIDTH), lambda i: (i, 0)),
        ],
        out_shape=[
            jax.ShapeDtypeStruct((batch, A_HEADS, seq, LANES), bf16),
            jax.ShapeDtypeStruct((batch, A_HEADS, seq, LANES), bf16),
            jax.ShapeDtypeStruct((batch, A_HEADS, V_ROWS, seq), bf16),
            jax.ShapeDtypeStruct((n_tok, A_WIDTH), bf16),
            jax.ShapeDtypeStruct((n_tok, B_WIDTH), bf16),
        ],
        scratch_shapes=[pltpu.VMEM((D_MODEL, D_IN_EXT), bf16)],
        compiler_params=pltpu.CompilerParams(dimension_semantics=("arbitrary",), vmem_limit_bytes=VMEM_LIMIT),
        name="proj",
    )(x2, pos2, norm_in_g.reshape(1, D_MODEL), w_in_ext, q_lora_g.reshape(1, -1), w_uq_ext, w_uq_swap,
      kv_lora_g.reshape(1, -1), w_ukv.astype(bf16), qhg, khg, v_gate_g.reshape(1, B_WIDTH), ind, wpair, bpair,
      out_b_g.reshape(1, B_WIDTH), invf, sgn)

    q_tiles = seq // Q_TILE
    tok_blk = lambda w: pl.BlockSpec((Q_TILE, w), lambda b, i: (b * q_tiles + i, 0))
    resident = pl.Buffered(1)
    y = pl.pallas_call(
        _attn_body,
        grid=(batch, q_tiles),
        in_specs=[
            pl.BlockSpec((1, A_HEADS, Q_TILE, LANES), lambda b, i: (b, 0, i, 0)),
            pl.BlockSpec((1, A_HEADS, seq, LANES), lambda b, i: (b, 0, 0, 0)),
            pl.BlockSpec((1, A_HEADS, V_ROWS, seq), lambda b, i: (b, 0, 0, 0)),
            tok_blk(A_WIDTH), tok_blk(B_WIDTH), tok_blk(D_MODEL),
            pl.BlockSpec((1, A_WIDTH), lambda b, i: (0, 0), pipeline_mode=resident),
            pl.BlockSpec((D_MODEL, D_MODEL), lambda b, i: (0, 0), pipeline_mode=resident),
        ],
        out_specs=tok_blk(D_MODEL),
        out_shape=jax.ShapeDtypeStruct((n_tok, D_MODEL), f32),
        scratch_shapes=[pltpu.VMEM((seq, Q_TILE), f32), pltpu.VMEM((seq, Q_TILE), f32),
                        pltpu.VMEM((A_HEADS, V_ROWS, Q_TILE), f32)],
        compiler_params=pltpu.CompilerParams(dimension_semantics=("arbitrary", "arbitrary"),
                                             vmem_limit_bytes=VMEM_LIMIT),
        name="attn",
    )(q, k, v, ga, mb, x2, out_a_g.reshape(1, A_WIDTH), w_out.astype(bf16))
    return y.reshape(batch, seq, D_MODEL)
```

```python
import math

import jax
import jax.numpy as jnp
import numpy as np
from jax import lax
from jax.experimental import pallas as pl
from jax.experimental.pallas import tpu as pltpu

D_MODEL = 1024
A_HEADS = 8
A_V_DIM = 64
A_WIDTH = A_HEADS * A_V_DIM
QK_NOPE_DIM = 64
QK_ROPE_DIM = 32
HALF_ROPE = QK_ROPE_DIM // 2
QK_DIM = QK_NOPE_DIM + QK_ROPE_DIM
Q_LORA_RANK = 256
KV_LORA_RANK = 128
ROPE_THETA = 10000.0
B_HEADS = 8
B_HEAD_DIM = 64
B_WIDTH = B_HEADS * B_HEAD_DIM
CHUNK = 128
EPS = 1e-6

LANES = 128
SUBLANES = 8
HEAD_W = A_HEADS * LANES
V_ROWS = 80

OFF_CQ = 0
OFF_CKV = OFF_CQ + Q_LORA_RANK
OFF_KR = OFF_CKV + KV_LORA_RANK
OFF_ZA = OFF_KR + LANES
OFF_U = OFF_ZA + A_WIDTH
OFF_V = OFF_U + B_WIDTH
OFF_ZB = OFF_V + B_WIDTH
D_IN_EXT = OFF_ZB + B_WIDTH

TOK_TILE = 512
W_T_BLOCK = 256
VEC_QHG, VEC_KHG, VEC_INVF, VEC_VGG = 0, LANES, 2 * LANES, 3 * LANES
VEC_WIDTH = VEC_VGG + B_WIDTH
ROPE_ROWS = TOK_TILE * QK_ROPE_DIM // LANES
Q_TILE = 512
KV_TILE = 1024
VMEM_LIMIT = 63 * 1024 * 1024

f32 = jnp.float32
bf16 = jnp.bfloat16


def _gelu(t):
    return jax.nn.gelu(t)


def _silu(t):
    half = 0.5 * t
    return half + half * jnp.tanh(half)


def _inv_norm(t, width):
    return lax.rsqrt(jnp.sum(t * t, axis=-1, keepdims=True) + width * EPS)


def _proj_body(x_ref, pos_ref, g_in_ref, w_in_ref, gq_ref, w_uq_ref, gkv_ref, w_ukv_ref,
               vec_ref, ind_ref, wpair_ref, bpair_ref, gob_ref, sgn_ref,
               q_out, k_out, v_out, ga_out, mb_out, w_in_scr):
    @pl.when(pl.program_id(0) == 0)
    def _():
        for lo in range(0, D_IN_EXT, W_T_BLOCK):
            w_in_scr[:, lo:lo + W_T_BLOCK] = w_in_ref[lo:lo + W_T_BLOCK, :].T

    x = x_ref[...]
    h = (x * _inv_norm(x, D_MODEL) * (g_in_ref[...] * math.sqrt(D_MODEL))).astype(bf16)

    def proj(lo, width):
        return jnp.dot(h, w_in_scr[:, lo:lo + width], preferred_element_type=f32)

    c_q = proj(OFF_CQ, Q_LORA_RANK)
    c_kv = proj(OFF_CKV, KV_LORA_RANK)
    kr = proj(OFF_KR, LANES)

    lane = lax.broadcasted_iota(jnp.int32, (1, LANES), 1)
    rope_lanes = (lane >= QK_NOPE_DIM) & (lane < QK_DIM)
    ang = pos_ref[...].astype(f32) * vec_ref[:, VEC_INVF:VEC_INVF + LANES]
    cos_c, sin_c = jnp.cos(ang), jnp.sin(ang)
    cos_rows, sin_rows = [], []
    for g in range(LANES // QK_ROPE_DIM):
        shift = (QK_NOPE_DIM - QK_ROPE_DIM * g) % LANES
        cos_rows.append(pltpu.roll(cos_c, shift, 1) if shift else cos_c)
        sin_rows.append(pltpu.roll(sin_c, shift, 1) if shift else sin_c)
    cos = jnp.where(rope_lanes, jnp.concatenate(cos_rows, axis=0), 1.0)
    first = (lane >= QK_NOPE_DIM) & (lane < QK_NOPE_DIM + HALF_ROPE)
    sin_signed = jnp.concatenate(sin_rows, axis=0) * sgn_ref[...]

    def rope(t):
        swapped = jnp.where(first, pltpu.roll(t, LANES - HALF_ROPE, 1), pltpu.roll(t, HALF_ROPE, 1))
        return t * cos + swapped * sin_signed

    cqn = (c_q * _inv_norm(c_q, Q_LORA_RANK) * (gq_ref[...] * math.sqrt(Q_LORA_RANK))).astype(bf16)
    qhg = vec_ref[:, VEC_QHG:VEC_QHG + LANES]
    for pair in range(A_HEADS // 2):
        cols = slice(2 * pair * LANES, 2 * (pair + 1) * LANES)
        qq = jnp.dot(cqn, w_uq_ref[:, cols], preferred_element_type=f32)
        swap_cols = slice(HEAD_W + 2 * pair * LANES, HEAD_W + 2 * (pair + 1) * LANES)
        qs = jnp.dot(cqn, w_uq_ref[:, swap_cols], preferred_element_type=f32)
        for side in range(2):
            lanes = slice(side * LANES, (side + 1) * LANES)
            qh = qq[:, lanes] * cos + qs[:, lanes] * sin_signed
            q_out[0, 2 * pair + side] = (qh * _inv_norm(qh, QK_DIM) * qhg).astype(bf16)

    ckvn = (c_kv * _inv_norm(c_kv, KV_LORA_RANK) * (gkv_ref[...] * math.sqrt(KV_LORA_RANK))).astype(bf16)
    kpe = jnp.where(rope_lanes, rope(kr), 0.0)
    khg = vec_ref[:, VEC_KHG:VEC_KHG + LANES]
    nope_lanes = lane < QK_NOPE_DIM
    tail_row = lax.broadcasted_iota(jnp.int32, (V_ROWS - A_V_DIM, TOK_TILE), 0)
    v_tail = jnp.where(tail_row == 0, 1.0, 0.0).astype(bf16)
    for pair in range(A_HEADS // 2):
        kv = jnp.dot(ckvn, w_ukv_ref[:, 2 * pair * LANES:2 * (pair + 1) * LANES], preferred_element_type=f32)
        for side in range(2):
            hd = 2 * pair + side
            group = kv[:, side * LANES:(side + 1) * LANES]
            kh = jnp.where(nope_lanes, group, kpe)
            k_out[0, hd] = (kh * _inv_norm(kh, QK_DIM) * khg).astype(bf16)
            v_out[0, hd, 0:A_V_DIM, :] = group.T[QK_NOPE_DIM:].astype(bf16)
            v_out[0, hd, A_V_DIM:, :] = v_tail

    u = _gelu(proj(OFF_U, B_WIDTH))
    v = _gelu(proj(OFF_V, B_WIDTH))
    gate_b = _silu(proj(OFF_ZB, B_WIDTH))
    ssq = jnp.dot((v * v).astype(bf16), ind_ref[...], preferred_element_type=f32)
    ga_out[...] = _silu(proj(OFF_ZA, A_WIDTH)).astype(bf16)
    vn = (v * lax.rsqrt(ssq + B_HEAD_DIM * EPS) * (vec_ref[:, VEC_VGG:VEC_VGG + B_WIDTH] * math.sqrt(B_HEAD_DIM))).astype(bf16)
    lower = lane < B_HEAD_DIM
    zero = jnp.zeros((), bf16)
    chunks = []
    for c in range(0, TOK_TILE // CHUNK, 2):
        pairs = [[], []]
        for p in range(B_HEADS // 2):
            lanes = slice(p * LANES, (p + 1) * LANES)
            rhs = []
            for cc in (c, c + 1):
                vp = vn[cc * CHUNK:(cc + 1) * CHUNK, lanes]
                rhs.append(jnp.concatenate([jnp.where(lower, vp, zero), jnp.where(lower, zero, vp)], axis=0))
            sv = jnp.dot(wpair_ref[p], jnp.concatenate(rhs, axis=1), preferred_element_type=f32)
            for k, cc in enumerate((c, c + 1)):
                pairs[k].append(u[cc * CHUNK:(cc + 1) * CHUNK, lanes] * (sv[:, k * LANES:(k + 1) * LANES] + bpair_ref[p]))
        chunks += [jnp.concatenate(pairs[0], axis=1), jnp.concatenate(pairs[1], axis=1)]
    ob = jnp.concatenate(chunks, axis=0)
    obn = ob * _inv_norm(ob, B_WIDTH) * (gob_ref[...] * math.sqrt(B_WIDTH))
    mb_out[...] = (obn * gate_b).astype(bf16)


def _attn_body(q_ref, k_ref, v_ref, ga_ref, mb_ref, x_ref, goa_ref, w_out_ref, y_ref,
               s_even, s_odd, acc_scr):
    seq = k_ref.shape[2]
    n_kv = seq // KV_TILE

    def head_step(hd, parity, m_prev, do_scores=True, do_weighted=True):
        s_cur, s_prev = (s_even, s_odd) if parity == 0 else (s_odd, s_even)
        q_t = q_ref[0, hd].T if do_scores else None
        m = None
        acc = [None, None]
        for t in range(n_kv):
            keys = slice(t * KV_TILE, (t + 1) * KV_TILE)
            if do_scores:
                s = jnp.dot(k_ref[0, hd, keys, :], q_t, preferred_element_type=f32)
                s_cur[keys, :] = s
                s_max = jnp.max(s.reshape(KV_TILE // SUBLANES, SUBLANES, s.shape[1]), axis=0)
                m = s_max if m is None else jnp.maximum(m, s_max)
            if do_weighted:
                p = jnp.exp2(s_prev[keys, :] - m_prev).astype(bf16)
                pv = jnp.dot(v_ref[0, hd - 1, :, keys], p, preferred_element_type=f32)
                acc[t % 2] = pv if acc[t % 2] is None else acc[t % 2] + pv
        if do_weighted:
            acc_scr[hd - 1] = acc[0] + acc[1]
        return jnp.max(m, axis=0, keepdims=True) if do_scores else None

    def pair_step(i, m_prev):
        m_odd = head_step(2 * i + 1, 1, m_prev)
        return head_step(2 * i + 2, 0, m_odd)

    m_cur = head_step(0, 0, None, do_weighted=False)
    m_cur = lax.fori_loop(0, (A_HEADS - 2) // 2, pair_step, m_cur)
    m_cur = head_step(A_HEADS - 1, 1, m_cur)
    half = D_MODEL // 2
    y_b = [x_ref[:, c:c + half] + jnp.dot(mb_ref[...], w_out_ref[A_WIDTH:, c:c + half], preferred_element_type=f32)
           for c in (0, half)]
    head_step(A_HEADS, 0, m_cur, do_scores=False)

    heads = []
    for hd in range(A_HEADS):
        acc = acc_scr[hd]
        heads.append(acc[0:A_V_DIM] / acc[A_V_DIM:A_V_DIM + 1])
    o = jnp.concatenate(heads, axis=0).T
    on = o * _inv_norm(o, A_WIDTH) * (goa_ref[...] * math.sqrt(A_WIDTH))
    mix_a = (on * ga_ref[...].astype(f32)).astype(bf16)
    for i, c in enumerate((0, half)):
        y_ref[:, c:c + half] = y_b[i] + jnp.dot(mix_a, w_out_ref[0:A_WIDTH, c:c + half], preferred_element_type=f32)


def _const_spec(shape):
    return pl.BlockSpec(shape, lambda *_: (0,) * len(shape))


def kernel(x, positions, norm_in_g, w_in, q_lora_g, w_uq, kv_lora_g, w_ukv, q_head_g, k_head_g,
           v_gate_g, w_s, b_s, out_a_g, out_b_g, w_out):
    batch, seq, _ = x.shape
    n_tok = batch * seq
    assert seq % Q_TILE == 0 and seq % KV_TILE == 0 and seq % TOK_TILE == 0 and TOK_TILE % CHUNK == 0

    w_t = w_in.T
    w_in_ext = jnp.concatenate(
        [w_t[:OFF_KR], jnp.zeros((QK_NOPE_DIM, D_MODEL), f32), w_t[OFF_KR:OFF_KR + QK_ROPE_DIM],
         jnp.zeros((LANES - QK_DIM, D_MODEL), f32), w_t[OFF_KR + QK_ROPE_DIM:]], axis=0).astype(bf16)

    uq = w_uq.reshape(Q_LORA_RANK, A_HEADS, QK_DIM)
    pad = jnp.zeros((Q_LORA_RANK, A_HEADS, LANES - QK_DIM), f32)
    uq_main = jnp.concatenate([uq, pad], axis=-1)
    uq_swap = jnp.concatenate([jnp.zeros_like(uq[..., :QK_NOPE_DIM]), uq[..., QK_NOPE_DIM + HALF_ROPE:],
                               uq[..., QK_NOPE_DIM:QK_NOPE_DIM + HALF_ROPE], pad], axis=-1)
    w_uq_ext = jnp.concatenate([uq_main, uq_swap], axis=1).reshape(Q_LORA_RANK, 2 * HEAD_W).astype(bf16)

    assert QK_NOPE_DIM + A_V_DIM == LANES

    qhg = jnp.pad(q_head_g * math.log2(math.e), (0, LANES - QK_DIM))
    khg = jnp.pad(k_head_g * math.sqrt(QK_DIM), (0, LANES - QK_DIM))

    inv_freq = 1.0 / (ROPE_THETA ** (jnp.arange(0, QK_ROPE_DIM, 2, dtype=f32) / QK_ROPE_DIM))
    invf = jnp.tile(inv_freq, LANES // HALF_ROPE)
    vec = jnp.concatenate([qhg, khg, invf, v_gate_g.reshape(B_WIDTH)]).reshape(1, VEC_WIDTH)
    sgn = np.zeros((1, LANES), np.float32)
    sgn[0, QK_NOPE_DIM:QK_NOPE_DIM + HALF_ROPE] = -1.0
    sgn[0, QK_NOPE_DIM + HALF_ROPE:QK_DIM] = 1.0
    sgn = jnp.asarray(sgn)

    head_of = np.arange(B_WIDTH) // B_HEAD_DIM
    ind = jnp.asarray((head_of[:, None] == head_of[None, :]).astype(np.float32)).astype(bf16)
    wpair = jnp.concatenate([w_s[0::2], w_s[1::2]], axis=2).astype(bf16)
    bpair = jnp.concatenate([jnp.broadcast_to(b_s[0::2, :, None], (B_HEADS // 2, CHUNK, B_HEAD_DIM)),
                             jnp.broadcast_to(b_s[1::2, :, None], (B_HEADS // 2, CHUNK, B_HEAD_DIM))], axis=2)

    x2 = x.reshape(n_tok, D_MODEL)
    groups = LANES // QK_ROPE_DIM
    pos2 = positions.reshape(n_tok // TOK_TILE, groups, ROPE_ROWS).transpose(0, 2, 1)
    pos2 = jnp.repeat(pos2, QK_ROPE_DIM, axis=2).reshape(n_tok // groups, LANES)
    tiles_per_row = seq // TOK_TILE

    head_out = lambda t: pl.BlockSpec((1, A_HEADS, t, LANES), lambda i: (i // tiles_per_row, 0, i % tiles_per_row, 0))
    q, k, v, ga, mb = pl.pallas_call(
        _proj_body,
        grid=(n_tok // TOK_TILE,),
        in_specs=[
            pl.BlockSpec((TOK_TILE, D_MODEL), lambda i: (i, 0)),
            pl.BlockSpec((ROPE_ROWS, LANES), lambda i: (i, 0)),
            _const_spec((1, D_MODEL)),
            _const_spec((D_IN_EXT, D_MODEL)),
            _const_spec((1, Q_LORA_RANK)),
            _const_spec((Q_LORA_RANK, 2 * HEAD_W)),
            _const_spec((1, KV_LORA_RANK)),
            _const_spec((KV_LORA_RANK, HEAD_W)),
            _const_spec((1, VEC_WIDTH)),
            _const_spec((B_WIDTH, B_WIDTH)),
            _const_spec((B_HEADS // 2, CHUNK, 2 * CHUNK)),
            _const_spec((B_HEADS // 2, CHUNK, LANES)),
            _const_spec((1, B_WIDTH)),
            _const_spec((1, LANES)),
        ],
        out_specs=[
            head_out(TOK_TILE), head_out(TOK_TILE),
            pl.BlockSpec((1, A_HEADS, V_ROWS, TOK_TILE), lambda i: (i // tiles_per_row, 0, 0, i % tiles_per_row)),
            pl.BlockSpec((TOK_TILE, A_WIDTH), lambda i: (i, 0)),
            pl.BlockSpec((TOK_TILE, B_WIDTH), lambda i: (i, 0)),
        ],
        out_shape=[
            jax.ShapeDtypeStruct((batch, A_HEADS, seq, LANES), bf16),
            jax.ShapeDtypeStruct((batch, A_HEADS, seq, LANES), bf16),
            jax.ShapeDtypeStruct((batch, A_HEADS, V_ROWS, seq), bf16),
            jax.ShapeDtypeStruct((n_tok, A_WIDTH), bf16),
            jax.ShapeDtypeStruct((n_tok, B_WIDTH), bf16),
        ],
        scratch_shapes=[pltpu.VMEM((D_MODEL, D_IN_EXT), bf16)],
        compiler_params=pltpu.CompilerParams(dimension_semantics=("arbitrary",), vmem_limit_bytes=VMEM_LIMIT),
        name="proj",
    )(x2, pos2, norm_in_g.reshape(1, D_MODEL), w_in_ext, q_lora_g.reshape(1, -1), w_uq_ext,
      kv_lora_g.reshape(1, -1), w_ukv.astype(bf16), vec, ind, wpair, bpair, out_b_g.reshape(1, B_WIDTH), sgn)

    q_tiles = seq // Q_TILE
    tok_blk = lambda w: pl.BlockSpec((Q_TILE, w), lambda b, i: (b * q_tiles + i, 0))
    resident = pl.Buffered(1)
    y = pl.pallas_call(
        _attn_body,
        grid=(batch, q_tiles),
        in_specs=[
            pl.BlockSpec((1, A_HEADS, Q_TILE, LANES), lambda b, i: (b, 0, i, 0)),
            pl.BlockSpec((1, A_HEADS, seq, LANES), lambda b, i: (b, 0, 0, 0)),
            pl.BlockSpec((1, A_HEADS, V_ROWS, seq), lambda b, i: (b, 0, 0, 0)),
            tok_blk(A_WIDTH), tok_blk(B_WIDTH), tok_blk(D_MODEL),
            pl.BlockSpec((1, A_WIDTH), lambda b, i: (0, 0), pipeline_mode=resident),
            pl.BlockSpec((D_MODEL, D_MODEL), lambda b, i: (0, 0), pipeline_mode=resident),
        ],
        out_specs=tok_blk(D_MODEL),
        out_shape=jax.ShapeDtypeStruct((n_tok, D_MODEL), f32),
        scratch_shapes=[pltpu.VMEM((seq, Q_TILE), f32), pltpu.VMEM((seq, Q_TILE), f32),
                        pltpu.VMEM((A_HEADS, V_ROWS, Q_TILE), f32)],
        compiler_params=pltpu.CompilerParams(dimension_semantics=("arbitrary", "arbitrary"),
                                             vmem_limit_bytes=VMEM_LIMIT),
        name="attn",
    )(q, k, v, ga, mb, x2, out_a_g.reshape(1, A_WIDTH), w_out.astype(bf16))
    return y.reshape(batch, seq, D_MODEL)
```

```python
import math

import jax
import jax.numpy as jnp
import numpy as np
from jax import lax
from jax.experimental import pallas as pl
from jax.experimental.pallas import tpu as pltpu

D_MODEL = 1024
A_HEADS = 8
A_V_DIM = 64
A_WIDTH = A_HEADS * A_V_DIM
QK_NOPE_DIM = 64
QK_ROPE_DIM = 32
HALF_ROPE = QK_ROPE_DIM // 2
QK_DIM = QK_NOPE_DIM + QK_ROPE_DIM
Q_LORA_RANK = 256
KV_LORA_RANK = 128
ROPE_THETA = 10000.0
B_HEADS = 8
B_HEAD_DIM = 64
B_WIDTH = B_HEADS * B_HEAD_DIM
CHUNK = 128
EPS = 1e-6

LANES = 128
SUBLANES = 8
HEAD_W = A_HEADS * LANES
V_ROWS = 80

OFF_CQ = 0
OFF_CKV = OFF_CQ + Q_LORA_RANK
OFF_KR = OFF_CKV + KV_LORA_RANK
OFF_ZA = OFF_KR + LANES
OFF_U = OFF_ZA + A_WIDTH
OFF_V = OFF_U + B_WIDTH
OFF_ZB = OFF_V + B_WIDTH
D_IN_EXT = OFF_ZB + B_WIDTH

TOK_TILE = 512
W_T_BLOCK = 256
VEC_QHG, VEC_KHG, VEC_INVF, VEC_VGG = 0, LANES, 2 * LANES, 3 * LANES
VEC_WIDTH = VEC_VGG + B_WIDTH
ROPE_ROWS = TOK_TILE * QK_ROPE_DIM // LANES
Q_TILE = 512
KV_TILE = 1024
VMEM_LIMIT = 63 * 1024 * 1024
MAX_SCORE_BOUND = 48.0

f32 = jnp.float32
bf16 = jnp.bfloat16


def _gelu(t):
    return jax.nn.gelu(t)


def _silu(t):
    half = 0.5 * t
    return half + half * jnp.tanh(half)


def _inv_norm(t, width):
    return lax.rsqrt(jnp.sum(t * t, axis=-1, keepdims=True) + width * EPS)


def _proj_body(x_ref, pos_ref, g_in_ref, w_in_ref, gq_ref, w_uq_ref, gkv_ref, w_ukv_ref,
               vec_ref, ind_ref, wpair_ref, bpair_ref, gob_ref, sgn_ref,
               q_out, k_out, v_out, ga_out, mb_out, w_in_scr):
    @pl.when(pl.program_id(0) == 0)
    def _():
        for lo in range(0, D_IN_EXT, W_T_BLOCK):
            w_in_scr[:, lo:lo + W_T_BLOCK] = w_in_ref[lo:lo + W_T_BLOCK, :].T

    x = x_ref[...]
    h = (x * _inv_norm(x, D_MODEL) * (g_in_ref[...] * math.sqrt(D_MODEL))).astype(bf16)

    def proj(lo, width):
        return jnp.dot(h, w_in_scr[:, lo:lo + width], preferred_element_type=f32)

    c_q = proj(OFF_CQ, Q_LORA_RANK)
    c_kv = proj(OFF_CKV, KV_LORA_RANK)
    kr = proj(OFF_KR, LANES)

    lane = lax.broadcasted_iota(jnp.int32, (1, LANES), 1)
    rope_lanes = (lane >= QK_NOPE_DIM) & (lane < QK_DIM)
    ang = pos_ref[...].astype(f32) * vec_ref[:, VEC_INVF:VEC_INVF + LANES]
    cos_c, sin_c = jnp.cos(ang), jnp.sin(ang)
    cos_rows, sin_rows = [], []
    for g in range(LANES // QK_ROPE_DIM):
        shift = (QK_NOPE_DIM - QK_ROPE_DIM * g) % LANES
        cos_rows.append(pltpu.roll(cos_c, shift, 1) if shift else cos_c)
        sin_rows.append(pltpu.roll(sin_c, shift, 1) if shift else sin_c)
    cos = jnp.where(rope_lanes, jnp.concatenate(cos_rows, axis=0), 1.0)
    first = (lane >= QK_NOPE_DIM) & (lane < QK_NOPE_DIM + HALF_ROPE)
    sin_signed = jnp.concatenate(sin_rows, axis=0) * sgn_ref[...]

    def rope(t):
        swapped = jnp.where(first, pltpu.roll(t, LANES - HALF_ROPE, 1), pltpu.roll(t, HALF_ROPE, 1))
        return t * cos + swapped * sin_signed

    cqn = (c_q * _inv_norm(c_q, Q_LORA_RANK) * (gq_ref[...] * math.sqrt(Q_LORA_RANK))).astype(bf16)
    qhg = vec_ref[:, VEC_QHG:VEC_QHG + LANES]
    for pair in range(A_HEADS // 2):
        cols = slice(2 * pair * LANES, 2 * (pair + 1) * LANES)
        qq = jnp.dot(cqn, w_uq_ref[:, cols], preferred_element_type=f32)
        swap_cols = slice(HEAD_W + 2 * pair * LANES, HEAD_W + 2 * (pair + 1) * LANES)
        qs = jnp.dot(cqn, w_uq_ref[:, swap_cols], preferred_element_type=f32)
        for side in range(2):
            lanes = slice(side * LANES, (side + 1) * LANES)
            qh = qq[:, lanes] * cos + qs[:, lanes] * sin_signed
            q_out[0, 2 * pair + side] = (qh * _inv_norm(qh, QK_DIM) * qhg).astype(bf16)

    ckvn = (c_kv * _inv_norm(c_kv, KV_LORA_RANK) * (gkv_ref[...] * math.sqrt(KV_LORA_RANK))).astype(bf16)
    kpe = jnp.where(rope_lanes, rope(kr), 0.0)
    khg = vec_ref[:, VEC_KHG:VEC_KHG + LANES]
    nope_lanes = lane < QK_NOPE_DIM
    tail_row = lax.broadcasted_iota(jnp.int32, (V_ROWS - A_V_DIM, TOK_TILE), 0)
    v_tail = jnp.where(tail_row == 0, 1.0, 0.0).astype(bf16)
    for pair in range(A_HEADS // 2):
        kv = jnp.dot(ckvn, w_ukv_ref[:, 2 * pair * LANES:2 * (pair + 1) * LANES], preferred_element_type=f32)
        for side in range(2):
            hd = 2 * pair + side
            group = kv[:, side * LANES:(side + 1) * LANES]
            kh = jnp.where(nope_lanes, group, kpe)
            k_out[0, hd] = (kh * _inv_norm(kh, QK_DIM) * khg).astype(bf16)
            v_out[0, hd, 0:A_V_DIM, :] = group.T[QK_NOPE_DIM:].astype(bf16)
            v_out[0, hd, A_V_DIM:, :] = v_tail

    u = _gelu(proj(OFF_U, B_WIDTH))
    v = _gelu(proj(OFF_V, B_WIDTH))
    gate_b = _silu(proj(OFF_ZB, B_WIDTH))
    ssq = jnp.dot((v * v).astype(bf16), ind_ref[...], preferred_element_type=f32)
    ga_out[...] = _silu(proj(OFF_ZA, A_WIDTH)).astype(bf16)
    vn = (v * lax.rsqrt(ssq + B_HEAD_DIM * EPS) * (vec_ref[:, VEC_VGG:VEC_VGG + B_WIDTH] * math.sqrt(B_HEAD_DIM))).astype(bf16)
    lower = lane < B_HEAD_DIM
    zero = jnp.zeros((), bf16)
    chunks = []
    for c in range(0, TOK_TILE // CHUNK, 2):
        pairs = [[], []]
        for p in range(B_HEADS // 2):
            lanes = slice(p * LANES, (p + 1) * LANES)
            rhs = []
            for cc in (c, c + 1):
                vp = vn[cc * CHUNK:(cc + 1) * CHUNK, lanes]
                rhs.append(jnp.concatenate([jnp.where(lower, vp, zero), jnp.where(lower, zero, vp)], axis=0))
            sv = jnp.dot(wpair_ref[p], jnp.concatenate(rhs, axis=1), preferred_element_type=f32)
            for k, cc in enumerate((c, c + 1)):
                pairs[k].append(u[cc * CHUNK:(cc + 1) * CHUNK, lanes] * (sv[:, k * LANES:(k + 1) * LANES] + bpair_ref[p]))
        chunks += [jnp.concatenate(pairs[0], axis=1), jnp.concatenate(pairs[1], axis=1)]
    ob = jnp.concatenate(chunks, axis=0)
    obn = ob * _inv_norm(ob, B_WIDTH) * (gob_ref[...] * math.sqrt(B_WIDTH))
    mb_out[...] = (obn * gate_b).astype(bf16)


def _group_b_projection(mb_ref, x_ref, w_out_ref):
    half = D_MODEL // 2
    return [x_ref[:, c:c + half] + jnp.dot(mb_ref[...], w_out_ref[A_WIDTH:, c:c + half], preferred_element_type=f32)
            for c in (0, half)]


def _attn_epilogue(acc_scr, y_b, ga_ref, goa_ref, w_out_ref, y_ref):
    half = D_MODEL // 2
    heads = []
    for hd in range(A_HEADS):
        acc = acc_scr[hd]
        heads.append(acc[0:A_V_DIM] / acc[A_V_DIM:A_V_DIM + 1])
    o = jnp.concatenate(heads, axis=0).T
    on = o * _inv_norm(o, A_WIDTH) * (goa_ref[...] * math.sqrt(A_WIDTH))
    mix_a = (on * ga_ref[...].astype(f32)).astype(bf16)
    for i, c in enumerate((0, half)):
        y_ref[:, c:c + half] = y_b[i] + jnp.dot(mix_a, w_out_ref[0:A_WIDTH, c:c + half], preferred_element_type=f32)


def _attn_bounded_body(bound_ref, q_ref, k_ref, v_ref, ga_ref, mb_ref, x_ref, goa_ref, w_out_ref, y_ref, acc_scr):
    seq = k_ref.shape[2]
    n_kv = seq // KV_TILE
    bound = jnp.tile(bound_ref[...], (1, q_ref.shape[2] // LANES))

    def one_head(hd):
        q_t = q_ref[0, hd].T
        acc = [None, None]
        s_next = jnp.dot(k_ref[0, hd, 0:KV_TILE, :], q_t, preferred_element_type=f32)
        for t in range(n_kv):
            s = s_next
            if t + 1 < n_kv:
                s_next = jnp.dot(k_ref[0, hd, (t + 1) * KV_TILE:(t + 2) * KV_TILE, :], q_t, preferred_element_type=f32)
            p = jnp.exp2(s - bound).astype(bf16)
            pv = jnp.dot(v_ref[0, hd, :, t * KV_TILE:(t + 1) * KV_TILE], p, preferred_element_type=f32)
            acc[t % 2] = pv if acc[t % 2] is None else acc[t % 2] + pv
        acc_scr[hd] = acc[0] + acc[1]

    def pair_step(i, carry):
        one_head(2 * i)
        one_head(2 * i + 1)
        return carry

    lax.fori_loop(0, A_HEADS // 2, pair_step, 0)
    _attn_epilogue(acc_scr, _group_b_projection(mb_ref, x_ref, w_out_ref), ga_ref, goa_ref, w_out_ref, y_ref)


def _attn_general_body(q_ref, k_ref, v_ref, ga_ref, mb_ref, x_ref, goa_ref, w_out_ref, y_ref,
                       s_even, s_odd, acc_scr):
    seq = k_ref.shape[2]
    n_kv = seq // KV_TILE

    def head_step(hd, parity, m_prev, do_scores=True, do_weighted=True):
        s_cur, s_prev = (s_even, s_odd) if parity == 0 else (s_odd, s_even)
        q_t = q_ref[0, hd].T if do_scores else None
        m = None
        acc = [None, None]
        for t in range(n_kv):
            keys = slice(t * KV_TILE, (t + 1) * KV_TILE)
            if do_scores:
                s = jnp.dot(k_ref[0, hd, keys, :], q_t, preferred_element_type=f32)
                s_cur[keys, :] = s
                s_max = jnp.max(s.reshape(KV_TILE // SUBLANES, SUBLANES, s.shape[1]), axis=0)
                m = s_max if m is None else jnp.maximum(m, s_max)
            if do_weighted:
                p = jnp.exp2(s_prev[keys, :] - m_prev).astype(bf16)
                pv = jnp.dot(v_ref[0, hd - 1, :, keys], p, preferred_element_type=f32)
                acc[t % 2] = pv if acc[t % 2] is None else acc[t % 2] + pv
        if do_weighted:
            acc_scr[hd - 1] = acc[0] + acc[1]
        return jnp.max(m, axis=0, keepdims=True) if do_scores else None

    def pair_step(i, m_prev):
        m_odd = head_step(2 * i + 1, 1, m_prev)
        return head_step(2 * i + 2, 0, m_odd)

    m_cur = head_step(0, 0, None, do_weighted=False)
    m_cur = lax.fori_loop(0, (A_HEADS - 2) // 2, pair_step, m_cur)
    m_cur = head_step(A_HEADS - 1, 1, m_cur)
    y_b = _group_b_projection(mb_ref, x_ref, w_out_ref)
    head_step(A_HEADS, 0, m_cur, do_scores=False)
    _attn_epilogue(acc_scr, y_b, ga_ref, goa_ref, w_out_ref, y_ref)


def _const_spec(shape):
    return pl.BlockSpec(shape, lambda *_: (0,) * len(shape))


def kernel(x, positions, norm_in_g, w_in, q_lora_g, w_uq, kv_lora_g, w_ukv, q_head_g, k_head_g,
           v_gate_g, w_s, b_s, out_a_g, out_b_g, w_out):
    batch, seq, _ = x.shape
    n_tok = batch * seq
    assert seq % Q_TILE == 0 and seq % KV_TILE == 0 and seq % TOK_TILE == 0 and TOK_TILE % CHUNK == 0

    w_t = w_in.T
    w_in_ext = jnp.concatenate(
        [w_t[:OFF_KR], jnp.zeros((QK_NOPE_DIM, D_MODEL), f32), w_t[OFF_KR:OFF_KR + QK_ROPE_DIM],
         jnp.zeros((LANES - QK_DIM, D_MODEL), f32), w_t[OFF_KR + QK_ROPE_DIM:]], axis=0).astype(bf16)

    uq = w_uq.reshape(Q_LORA_RANK, A_HEADS, QK_DIM)
    pad = jnp.zeros((Q_LORA_RANK, A_HEADS, LANES - QK_DIM), f32)
    uq_main = jnp.concatenate([uq, pad], axis=-1)
    uq_swap = jnp.concatenate([jnp.zeros_like(uq[..., :QK_NOPE_DIM]), uq[..., QK_NOPE_DIM + HALF_ROPE:],
                               uq[..., QK_NOPE_DIM:QK_NOPE_DIM + HALF_ROPE], pad], axis=-1)
    w_uq_ext = jnp.concatenate([uq_main, uq_swap], axis=1).reshape(Q_LORA_RANK, 2 * HEAD_W).astype(bf16)

    assert QK_NOPE_DIM + A_V_DIM == LANES

    qhg = jnp.pad(q_head_g * math.log2(math.e), (0, LANES - QK_DIM))
    khg = jnp.pad(k_head_g * math.sqrt(QK_DIM), (0, LANES - QK_DIM))

    inv_freq = 1.0 / (ROPE_THETA ** (jnp.arange(0, QK_ROPE_DIM, 2, dtype=f32) / QK_ROPE_DIM))
    invf = jnp.tile(inv_freq, LANES // HALF_ROPE)
    vec = jnp.concatenate([qhg, khg, invf, v_gate_g.reshape(B_WIDTH)]).reshape(1, VEC_WIDTH)
    sgn = np.zeros((1, LANES), np.float32)
    sgn[0, QK_NOPE_DIM:QK_NOPE_DIM + HALF_ROPE] = -1.0
    sgn[0, QK_NOPE_DIM + HALF_ROPE:QK_DIM] = 1.0
    sgn = jnp.asarray(sgn)

    head_of = np.arange(B_WIDTH) // B_HEAD_DIM
    ind = jnp.asarray((head_of[:, None] == head_of[None, :]).astype(np.float32)).astype(bf16)
    wpair = jnp.concatenate([w_s[0::2], w_s[1::2]], axis=2).astype(bf16)
    bpair = jnp.concatenate([jnp.broadcast_to(b_s[0::2, :, None], (B_HEADS // 2, CHUNK, B_HEAD_DIM)),
                             jnp.broadcast_to(b_s[1::2, :, None], (B_HEADS // 2, CHUNK, B_HEAD_DIM))], axis=2)

    x2 = x.reshape(n_tok, D_MODEL)
    groups = LANES // QK_ROPE_DIM
    pos2 = positions.reshape(n_tok // TOK_TILE, groups, ROPE_ROWS).transpose(0, 2, 1)
    pos2 = jnp.repeat(pos2, QK_ROPE_DIM, axis=2).reshape(n_tok // groups, LANES)
    tiles_per_row = seq // TOK_TILE

    head_out = lambda t: pl.BlockSpec((1, A_HEADS, t, LANES), lambda i: (i // tiles_per_row, 0, i % tiles_per_row, 0))
    q, k, v, ga, mb = pl.pallas_call(
        _proj_body,
        grid=(n_tok // TOK_TILE,),
        in_specs=[
            pl.BlockSpec((TOK_TILE, D_MODEL), lambda i: (i, 0)),
            pl.BlockSpec((ROPE_ROWS, LANES), lambda i: (i, 0)),
            _const_spec((1, D_MODEL)),
            _const_spec((D_IN_EXT, D_MODEL)),
            _const_spec((1, Q_LORA_RANK)),
            _const_spec((Q_LORA_RANK, 2 * HEAD_W)),
            _const_spec((1, KV_LORA_RANK)),
            _const_spec((KV_LORA_RANK, HEAD_W)),
            _const_spec((1, VEC_WIDTH)),
            _const_spec((B_WIDTH, B_WIDTH)),
            _const_spec((B_HEADS // 2, CHUNK, 2 * CHUNK)),
            _const_spec((B_HEADS // 2, CHUNK, LANES)),
            _const_spec((1, B_WIDTH)),
            _const_spec((1, LANES)),
        ],
        out_specs=[
            head_out(TOK_TILE), head_out(TOK_TILE),
            pl.BlockSpec((1, A_HEADS, V_ROWS, TOK_TILE), lambda i: (i // tiles_per_row, 0, 0, i % tiles_per_row)),
            pl.BlockSpec((TOK_TILE, A_WIDTH), lambda i: (i, 0)),
            pl.BlockSpec((TOK_TILE, B_WIDTH), lambda i: (i, 0)),
        ],
        out_shape=[
            jax.ShapeDtypeStruct((batch, A_HEADS, seq, LANES), bf16),
            jax.ShapeDtypeStruct((batch, A_HEADS, seq, LANES), bf16),
            jax.ShapeDtypeStruct((batch, A_HEADS, V_ROWS, seq), bf16),
            jax.ShapeDtypeStruct((n_tok, A_WIDTH), bf16),
            jax.ShapeDtypeStruct((n_tok, B_WIDTH), bf16),
        ],
        scratch_shapes=[pltpu.VMEM((D_MODEL, D_IN_EXT), bf16)],
        compiler_params=pltpu.CompilerParams(dimension_semantics=("arbitrary",), vmem_limit_bytes=VMEM_LIMIT),
        name="proj",
    )(x2, pos2, norm_in_g.reshape(1, D_MODEL), w_in_ext, q_lora_g.reshape(1, -1), w_uq_ext,
      kv_lora_g.reshape(1, -1), w_ukv.astype(bf16), vec, ind, wpair, bpair, out_b_g.reshape(1, B_WIDTH), sgn)

    q_tiles = seq // Q_TILE
    tok_blk = lambda w: pl.BlockSpec((Q_TILE, w), lambda b, i: (b * q_tiles + i, 0))
    resident = pl.Buffered(1)
    attn_specs = [
        pl.BlockSpec((1, A_HEADS, Q_TILE, LANES), lambda b, i: (b, 0, i, 0)),
        pl.BlockSpec((1, A_HEADS, seq, LANES), lambda b, i: (b, 0, 0, 0)),
        pl.BlockSpec((1, A_HEADS, V_ROWS, seq), lambda b, i: (b, 0, 0, 0)),
        tok_blk(A_WIDTH), tok_blk(B_WIDTH), tok_blk(D_MODEL),
        pl.BlockSpec((1, A_WIDTH), lambda b, i: (0, 0), pipeline_mode=resident),
        pl.BlockSpec((D_MODEL, D_MODEL), lambda b, i: (0, 0), pipeline_mode=resident),
    ]
    attn_args = (q, k, v, ga, mb, x2, out_a_g.reshape(1, A_WIDTH), w_out.astype(bf16))
    acc_scratch = pltpu.VMEM((A_HEADS, V_ROWS, Q_TILE), f32)

    def attention(body, extra_specs, scratch, name):
        return pl.pallas_call(
            body,
            grid=(batch, q_tiles),
            in_specs=extra_specs + attn_specs,
            out_specs=tok_blk(D_MODEL),
            out_shape=jax.ShapeDtypeStruct((n_tok, D_MODEL), f32),
            scratch_shapes=scratch,
            compiler_params=pltpu.CompilerParams(dimension_semantics=("arbitrary", "arbitrary"),
                                                 vmem_limit_bytes=VMEM_LIMIT),
            name=name,
        )

    score_bound = (jnp.max(jnp.abs(q_head_g)) * math.log2(math.e)) * (jnp.max(jnp.abs(k_head_g)) * math.sqrt(QK_DIM))
    score_bound = score_bound * (1.0 + 2.0 ** -6)
    y = lax.cond(
        score_bound <= MAX_SCORE_BOUND,
        lambda bound, *args: attention(_attn_bounded_body, [pl.BlockSpec((1, LANES), lambda b, i: (0, 0))],
                                       [acc_scratch], "attn_bounded")(jnp.full((1, LANES), bound, f32), *args),
        lambda bound, *args: attention(_attn_general_body, [],
                                       [pltpu.VMEM((seq, Q_TILE), f32), pltpu.VMEM((seq, Q_TILE), f32), acc_scratch],
                                       "attn")(*args),
        score_bound, *attn_args)
    return y.reshape(batch, seq, D_MODEL)
```

```python
import math

import jax
import jax.numpy as jnp
import numpy as np
from jax import lax
from jax.experimental import pallas as pl
from jax.experimental.pallas import tpu as pltpu

D_MODEL = 1024
A_HEADS = 8
A_V_DIM = 64
A_WIDTH = A_HEADS * A_V_DIM
QK_NOPE_DIM = 64
QK_ROPE_DIM = 32
HALF_ROPE = QK_ROPE_DIM // 2
QK_DIM = QK_NOPE_DIM + QK_ROPE_DIM
Q_LORA_RANK = 256
KV_LORA_RANK = 128
ROPE_THETA = 10000.0
B_HEADS = 8
B_HEAD_DIM = 64
B_WIDTH = B_HEADS * B_HEAD_DIM
CHUNK = 128
EPS = 1e-6

LANES = 128
SUBLANES = 8
HEAD_W = A_HEADS * LANES
V_ROWS = 80

OFF_CQ = 0
OFF_CKV = OFF_CQ + Q_LORA_RANK
OFF_KR = OFF_CKV + KV_LORA_RANK
OFF_ZA = OFF_KR + LANES
OFF_U = OFF_ZA + A_WIDTH
OFF_V = OFF_U + B_WIDTH
OFF_ZB = OFF_V + B_WIDTH
D_IN_EXT = OFF_ZB + B_WIDTH

TOK_TILE = 512
W_T_BLOCK = 256
VEC_QHG, VEC_KHG, VEC_INVF, VEC_VGG = 0, LANES, 2 * LANES, 3 * LANES
VEC_WIDTH = VEC_VGG + B_WIDTH
ROPE_ROWS = TOK_TILE * QK_ROPE_DIM // LANES
Q_TILE = 512
KV_TILE = 1024
VMEM_LIMIT = 63 * 1024 * 1024
MAX_SCORE_BOUND = 48.0

f32 = jnp.float32
bf16 = jnp.bfloat16


def _gelu(t):
    return jax.nn.gelu(t)


def _silu(t):
    half = 0.5 * t
    return half + half * jnp.tanh(half)


def _inv_norm(t, width):
    return lax.rsqrt(jnp.sum(t * t, axis=-1, keepdims=True) + width * EPS)


def _proj_body(x_ref, pos_ref, g_in_ref, w_in_ref, gq_ref, w_uq_ref, gkv_ref, w_ukv_ref,
               vec_ref, ind_ref, wpair_ref, bpair_ref, gob_ref, sgn_ref, w_out_b_ref,
               q_out, k_out, v_out, ga_out, yb_out, w_in_scr):
    @pl.when(pl.program_id(0) == 0)
    def _():
        for lo in range(0, D_IN_EXT, W_T_BLOCK):
            w_in_scr[:, lo:lo + W_T_BLOCK] = w_in_ref[lo:lo + W_T_BLOCK, :].T

    x = x_ref[...]
    h = (x * _inv_norm(x, D_MODEL) * (g_in_ref[...] * math.sqrt(D_MODEL))).astype(bf16)

    def proj(lo, width):
        return jnp.dot(h, w_in_scr[:, lo:lo + width], preferred_element_type=f32)

    c_q = proj(OFF_CQ, Q_LORA_RANK)
    c_kv = proj(OFF_CKV, KV_LORA_RANK)
    kr = proj(OFF_KR, LANES)

    lane = lax.broadcasted_iota(jnp.int32, (1, LANES), 1)
    rope_lanes = (lane >= QK_NOPE_DIM) & (lane < QK_DIM)
    ang = pos_ref[...].astype(f32) * vec_ref[:, VEC_INVF:VEC_INVF + LANES]
    cos_c, sin_c = jnp.cos(ang), jnp.sin(ang)
    cos_rows, sin_rows = [], []
    for g in range(LANES // QK_ROPE_DIM):
        shift = (QK_NOPE_DIM - QK_ROPE_DIM * g) % LANES
        cos_rows.append(pltpu.roll(cos_c, shift, 1) if shift else cos_c)
        sin_rows.append(pltpu.roll(sin_c, shift, 1) if shift else sin_c)
    cos = jnp.where(rope_lanes, jnp.concatenate(cos_rows, axis=0), 1.0)
    first = (lane >= QK_NOPE_DIM) & (lane < QK_NOPE_DIM + HALF_ROPE)
    sin_signed = jnp.concatenate(sin_rows, axis=0) * sgn_ref[...]

    def rope(t):
        swapped = jnp.where(first, pltpu.roll(t, LANES - HALF_ROPE, 1), pltpu.roll(t, HALF_ROPE, 1))
        return t * cos + swapped * sin_signed

    def queries():
        cqn = (c_q * _inv_norm(c_q, Q_LORA_RANK) * (gq_ref[...] * math.sqrt(Q_LORA_RANK))).astype(bf16)
        qhg = vec_ref[:, VEC_QHG:VEC_QHG + LANES]
        for pair in range(A_HEADS // 2):
            cols = slice(2 * pair * LANES, 2 * (pair + 1) * LANES)
            qq = jnp.dot(cqn, w_uq_ref[:, cols], preferred_element_type=f32)
            swap_cols = slice(HEAD_W + 2 * pair * LANES, HEAD_W + 2 * (pair + 1) * LANES)
            qs = jnp.dot(cqn, w_uq_ref[:, swap_cols], preferred_element_type=f32)
            for side in range(2):
                lanes = slice(side * LANES, (side + 1) * LANES)
                qh = qq[:, lanes] * cos + qs[:, lanes] * sin_signed
                q_out[0, 2 * pair + side] = (qh * _inv_norm(qh, QK_DIM) * qhg).astype(bf16)

    def keys_values():
        ckvn = (c_kv * _inv_norm(c_kv, KV_LORA_RANK) * (gkv_ref[...] * math.sqrt(KV_LORA_RANK))).astype(bf16)
        kpe = jnp.where(rope_lanes, rope(kr), 0.0)
        khg = vec_ref[:, VEC_KHG:VEC_KHG + LANES]
        nope_lanes = lane < QK_NOPE_DIM
        tail_row = lax.broadcasted_iota(jnp.int32, (V_ROWS - A_V_DIM, TOK_TILE), 0)
        v_tail = jnp.where(tail_row == 0, 1.0, 0.0).astype(bf16)
        for pair in range(A_HEADS // 2):
            kv = jnp.dot(ckvn, w_ukv_ref[:, 2 * pair * LANES:2 * (pair + 1) * LANES], preferred_element_type=f32)
            for side in range(2):
                hd = 2 * pair + side
                group = kv[:, side * LANES:(side + 1) * LANES]
                kh = jnp.where(nope_lanes, group, kpe)
                k_out[0, hd] = (kh * _inv_norm(kh, QK_DIM) * khg).astype(bf16)
                v_out[0, hd, 0:A_V_DIM, :] = group.T[QK_NOPE_DIM:].astype(bf16)
                v_out[0, hd, A_V_DIM:, :] = v_tail

    queries()
    keys_values()

    u = _gelu(proj(OFF_U, B_WIDTH))
    v = _gelu(proj(OFF_V, B_WIDTH))
    gate_b = _silu(proj(OFF_ZB, B_WIDTH))
    ssq = jnp.dot((v * v).astype(bf16), ind_ref[...], preferred_element_type=f32)
    ga_out[...] = _silu(proj(OFF_ZA, A_WIDTH)).astype(bf16)
    vn = (v * lax.rsqrt(ssq + B_HEAD_DIM * EPS) * (vec_ref[:, VEC_VGG:VEC_VGG + B_WIDTH] * math.sqrt(B_HEAD_DIM))).astype(bf16)
    lower = lane < B_HEAD_DIM
    zero = jnp.zeros((), bf16)
    chunks = []
    for c in range(0, TOK_TILE // CHUNK, 2):
        pairs = [[], []]
        for p in range(B_HEADS // 2):
            lanes = slice(p * LANES, (p + 1) * LANES)
            rhs = []
            for cc in (c, c + 1):
                vp = vn[cc * CHUNK:(cc + 1) * CHUNK, lanes]
                rhs.append(jnp.concatenate([jnp.where(lower, vp, zero), jnp.where(lower, zero, vp)], axis=0))
            sv = jnp.dot(wpair_ref[p], jnp.concatenate(rhs, axis=1), preferred_element_type=f32)
            for k, cc in enumerate((c, c + 1)):
                pairs[k].append(u[cc * CHUNK:(cc + 1) * CHUNK, lanes] * (sv[:, k * LANES:(k + 1) * LANES] + bpair_ref[p]))
        chunks += [jnp.concatenate(pairs[0], axis=1), jnp.concatenate(pairs[1], axis=1)]
    ob = jnp.concatenate(chunks, axis=0)
    obn = ob * _inv_norm(ob, B_WIDTH) * (gob_ref[...] * math.sqrt(B_WIDTH))
    mix_b = (obn * gate_b).astype(bf16)
    half = D_MODEL // 2
    for c in (0, half):
        yb_out[:, c:c + half] = x_ref[:, c:c + half] + jnp.dot(mix_b, w_out_b_ref[:, c:c + half],
                                                               preferred_element_type=f32)


def _attn_epilogue(acc_scr, yb_ref, ga_ref, goa_ref, w_out_ref, y_ref):
    half = D_MODEL // 2
    heads = []
    for hd in range(A_HEADS):
        acc = acc_scr[hd]
        heads.append(acc[0:A_V_DIM] / acc[A_V_DIM:A_V_DIM + 1])
    o = jnp.concatenate(heads, axis=0).T
    on = o * _inv_norm(o, A_WIDTH) * (goa_ref[...] * math.sqrt(A_WIDTH))
    mix_a = (on * ga_ref[...].astype(f32)).astype(bf16)
    for c in (0, half):
        y_ref[:, c:c + half] = yb_ref[:, c:c + half] + jnp.dot(mix_a, w_out_ref[:, c:c + half],
                                                               preferred_element_type=f32)


def _attn_bounded_body(bound_ref, q_ref, k_ref, v_ref, ga_ref, yb_ref, goa_ref, w_out_ref, y_ref, acc_scr):
    seq = k_ref.shape[2]
    n_kv = seq // KV_TILE
    bound = jnp.tile(bound_ref[...], (1, q_ref.shape[2] // LANES))

    def one_head(hd):
        q_t = q_ref[0, hd].T
        acc = [None, None]
        s_next = jnp.dot(k_ref[0, hd, 0:KV_TILE, :], q_t, preferred_element_type=f32)
        for t in range(n_kv):
            s = s_next
            if t + 1 < n_kv:
                s_next = jnp.dot(k_ref[0, hd, (t + 1) * KV_TILE:(t + 2) * KV_TILE, :], q_t, preferred_element_type=f32)
            p = jnp.exp2(s - bound).astype(bf16)
            pv = jnp.dot(v_ref[0, hd, :, t * KV_TILE:(t + 1) * KV_TILE], p, preferred_element_type=f32)
            acc[t % 2] = pv if acc[t % 2] is None else acc[t % 2] + pv
        acc_scr[hd] = acc[0] + acc[1]

    for hd in range(A_HEADS):
        one_head(hd)
    _attn_epilogue(acc_scr, yb_ref, ga_ref, goa_ref, w_out_ref, y_ref)


def _attn_general_body(q_ref, k_ref, v_ref, ga_ref, yb_ref, goa_ref, w_out_ref, y_ref,
                       s_even, s_odd, acc_scr):
    seq = k_ref.shape[2]
    n_kv = seq // KV_TILE

    def head_step(hd, parity, m_prev, do_scores=True, do_weighted=True):
        s_cur, s_prev = (s_even, s_odd) if parity == 0 else (s_odd, s_even)
        q_t = q_ref[0, hd].T if do_scores else None
        m = None
        acc = [None, None]
        for t in range(n_kv):
            keys = slice(t * KV_TILE, (t + 1) * KV_TILE)
            if do_scores:
                s = jnp.dot(k_ref[0, hd, keys, :], q_t, preferred_element_type=f32)
                s_cur[keys, :] = s
                s_max = jnp.max(s.reshape(KV_TILE // SUBLANES, SUBLANES, s.shape[1]), axis=0)
                m = s_max if m is None else jnp.maximum(m, s_max)
            if do_weighted:
                p = jnp.exp2(s_prev[keys, :] - m_prev).astype(bf16)
                pv = jnp.dot(v_ref[0, hd - 1, :, keys], p, preferred_element_type=f32)
                acc[t % 2] = pv if acc[t % 2] is None else acc[t % 2] + pv
        if do_weighted:
            acc_scr[hd - 1] = acc[0] + acc[1]
        return jnp.max(m, axis=0, keepdims=True) if do_scores else None

    def pair_step(i, m_prev):
        m_odd = head_step(2 * i + 1, 1, m_prev)
        return head_step(2 * i + 2, 0, m_odd)

    m_cur = head_step(0, 0, None, do_weighted=False)
    m_cur = lax.fori_loop(0, (A_HEADS - 2) // 2, pair_step, m_cur)
    m_cur = head_step(A_HEADS - 1, 1, m_cur)
    head_step(A_HEADS, 0, m_cur, do_scores=False)
    _attn_epilogue(acc_scr, yb_ref, ga_ref, goa_ref, w_out_ref, y_ref)


def _const_spec(shape):
    return pl.BlockSpec(shape, lambda *_: (0,) * len(shape))


def kernel(x, positions, norm_in_g, w_in, q_lora_g, w_uq, kv_lora_g, w_ukv, q_head_g, k_head_g,
           v_gate_g, w_s, b_s, out_a_g, out_b_g, w_out):
    batch, seq, _ = x.shape
    n_tok = batch * seq
    assert seq % Q_TILE == 0 and seq % KV_TILE == 0 and seq % TOK_TILE == 0 and TOK_TILE % CHUNK == 0

    w_t = w_in.T
    w_in_ext = jnp.concatenate(
        [w_t[:OFF_KR], jnp.zeros((QK_NOPE_DIM, D_MODEL), f32), w_t[OFF_KR:OFF_KR + QK_ROPE_DIM],
         jnp.zeros((LANES - QK_DIM, D_MODEL), f32), w_t[OFF_KR + QK_ROPE_DIM:]], axis=0).astype(bf16)

    uq = w_uq.reshape(Q_LORA_RANK, A_HEADS, QK_DIM)
    pad = jnp.zeros((Q_LORA_RANK, A_HEADS, LANES - QK_DIM), f32)
    uq_main = jnp.concatenate([uq, pad], axis=-1)
    uq_swap = jnp.concatenate([jnp.zeros_like(uq[..., :QK_NOPE_DIM]), uq[..., QK_NOPE_DIM + HALF_ROPE:],
                               uq[..., QK_NOPE_DIM:QK_NOPE_DIM + HALF_ROPE], pad], axis=-1)
    w_uq_ext = jnp.concatenate([uq_main, uq_swap], axis=1).reshape(Q_LORA_RANK, 2 * HEAD_W).astype(bf16)

    assert QK_NOPE_DIM + A_V_DIM == LANES

    qhg = jnp.pad(q_head_g * math.log2(math.e), (0, LANES - QK_DIM))
    khg = jnp.pad(k_head_g * math.sqrt(QK_DIM), (0, LANES - QK_DIM))

    inv_freq = 1.0 / (ROPE_THETA ** (jnp.arange(0, QK_ROPE_DIM, 2, dtype=f32) / QK_ROPE_DIM))
    invf = jnp.tile(inv_freq, LANES // HALF_ROPE)
    vec = jnp.concatenate([qhg, khg, invf, v_gate_g.reshape(B_WIDTH)]).reshape(1, VEC_WIDTH)
    sgn = np.zeros((1, LANES), np.float32)
    sgn[0, QK_NOPE_DIM:QK_NOPE_DIM + HALF_ROPE] = -1.0
    sgn[0, QK_NOPE_DIM + HALF_ROPE:QK_DIM] = 1.0
    sgn = jnp.asarray(sgn)

    head_of = np.arange(B_WIDTH) // B_HEAD_DIM
    ind = jnp.asarray((head_of[:, None] == head_of[None, :]).astype(np.float32)).astype(bf16)
    wpair = jnp.concatenate([w_s[0::2], w_s[1::2]], axis=2).astype(bf16)
    bpair = jnp.concatenate([jnp.broadcast_to(b_s[0::2, :, None], (B_HEADS // 2, CHUNK, B_HEAD_DIM)),
                             jnp.broadcast_to(b_s[1::2, :, None], (B_HEADS // 2, CHUNK, B_HEAD_DIM))], axis=2)

    x2 = x.reshape(n_tok, D_MODEL)
    groups = LANES // QK_ROPE_DIM
    pos2 = positions.reshape(n_tok // TOK_TILE, groups, ROPE_ROWS).transpose(0, 2, 1)
    pos2 = jnp.repeat(pos2, QK_ROPE_DIM, axis=2).reshape(n_tok // groups, LANES)
    tiles_per_row = seq // TOK_TILE

    w_out16 = w_out.astype(bf16)
    assert A_WIDTH == B_WIDTH
    head_out = lambda t: pl.BlockSpec((1, A_HEADS, t, LANES), lambda i: (i // tiles_per_row, 0, i % tiles_per_row, 0))
    q, k, v, ga, yb = pl.pallas_call(
        _proj_body,
        grid=(n_tok // TOK_TILE,),
        in_specs=[
            pl.BlockSpec((TOK_TILE, D_MODEL), lambda i: (i, 0)),
            pl.BlockSpec((ROPE_ROWS, LANES), lambda i: (i, 0)),
            _const_spec((1, D_MODEL)),
            _const_spec((D_IN_EXT, D_MODEL)),
            _const_spec((1, Q_LORA_RANK)),
            _const_spec((Q_LORA_RANK, 2 * HEAD_W)),
            _const_spec((1, KV_LORA_RANK)),
            _const_spec((KV_LORA_RANK, HEAD_W)),
            _const_spec((1, VEC_WIDTH)),
            _const_spec((B_WIDTH, B_WIDTH)),
            _const_spec((B_HEADS // 2, CHUNK, 2 * CHUNK)),
            _const_spec((B_HEADS // 2, CHUNK, LANES)),
            _const_spec((1, B_WIDTH)),
            _const_spec((1, LANES)),
            pl.BlockSpec((B_WIDTH, D_MODEL), lambda i: (1, 0)),
        ],
        out_specs=[
            head_out(TOK_TILE), head_out(TOK_TILE),
            pl.BlockSpec((1, A_HEADS, V_ROWS, TOK_TILE), lambda i: (i // tiles_per_row, 0, 0, i % tiles_per_row)),
            pl.BlockSpec((TOK_TILE, A_WIDTH), lambda i: (i, 0)),
            pl.BlockSpec((TOK_TILE, D_MODEL), lambda i: (i, 0)),
        ],
        out_shape=[
            jax.ShapeDtypeStruct((batch, A_HEADS, seq, LANES), bf16),
            jax.ShapeDtypeStruct((batch, A_HEADS, seq, LANES), bf16),
            jax.ShapeDtypeStruct((batch, A_HEADS, V_ROWS, seq), bf16),
            jax.ShapeDtypeStruct((n_tok, A_WIDTH), bf16),
            jax.ShapeDtypeStruct((n_tok, D_MODEL), f32),
        ],
        scratch_shapes=[pltpu.VMEM((D_MODEL, D_IN_EXT), bf16)],
        compiler_params=pltpu.CompilerParams(dimension_semantics=("arbitrary",), vmem_limit_bytes=VMEM_LIMIT),
        name="proj",
    )(x2, pos2, norm_in_g.reshape(1, D_MODEL), w_in_ext, q_lora_g.reshape(1, -1), w_uq_ext,
      kv_lora_g.reshape(1, -1), w_ukv.astype(bf16), vec, ind, wpair, bpair, out_b_g.reshape(1, B_WIDTH), sgn, w_out16)

    q_tiles = seq // Q_TILE
    tok_blk = lambda w: pl.BlockSpec((Q_TILE, w), lambda b, i: (b * q_tiles + i, 0))
    resident = pl.Buffered(1)
    attn_specs = [
        pl.BlockSpec((1, A_HEADS, Q_TILE, LANES), lambda b, i: (b, 0, i, 0)),
        pl.BlockSpec((1, A_HEADS, seq, LANES), lambda b, i: (b, 0, 0, 0)),
        pl.BlockSpec((1, A_HEADS, V_ROWS, seq), lambda b, i: (b, 0, 0, 0)),
        tok_blk(A_WIDTH), tok_blk(D_MODEL),
        pl.BlockSpec((1, A_WIDTH), lambda b, i: (0, 0), pipeline_mode=resident),
        pl.BlockSpec((A_WIDTH, D_MODEL), lambda b, i: (0, 0), pipeline_mode=resident),
    ]
    attn_args = (q, k, v, ga, yb, out_a_g.reshape(1, A_WIDTH), w_out16)
    acc_scratch = pltpu.VMEM((A_HEADS, V_ROWS, Q_TILE), f32)

    def attention(body, extra_specs, scratch, name):
        return pl.pallas_call(
            body,
            grid=(batch, q_tiles),
            in_specs=extra_specs + attn_specs,
            out_specs=tok_blk(D_MODEL),
            out_shape=jax.ShapeDtypeStruct((n_tok, D_MODEL), f32),
            scratch_shapes=scratch,
            compiler_params=pltpu.CompilerParams(dimension_semantics=("arbitrary", "arbitrary"),
                                                 vmem_limit_bytes=VMEM_LIMIT),
            name=name,
        )

    score_bound = (jnp.max(jnp.abs(q_head_g)) * math.log2(math.e)) * (jnp.max(jnp.abs(k_head_g)) * math.sqrt(QK_DIM))
    score_bound = score_bound * (1.0 + 2.0 ** -6)
    y = lax.cond(
        score_bound <= MAX_SCORE_BOUND,
        lambda bound, *args: attention(_attn_bounded_body, [pl.BlockSpec((1, LANES), lambda b, i: (0, 0))],
                                       [acc_scratch], "attn_bounded")(jnp.full((1, LANES), bound, f32), *args),
        lambda bound, *args: attention(_attn_general_body, [],
                                       [pltpu.VMEM((seq, Q_TILE), f32), pltpu.VMEM((seq, Q_TILE), f32), acc_scratch],
                                       "attn")(*args),
        score_bound, *attn_args)
    return y.reshape(batch, seq, D_MODEL)
```

```python
import math

import jax
import jax.numpy as jnp
import numpy as np
from jax import lax
from jax.experimental import pallas as pl
from jax.experimental.pallas import tpu as pltpu

D_MODEL = 1024
A_HEADS = 8
A_V_DIM = 64
A_WIDTH = A_HEADS * A_V_DIM
QK_NOPE_DIM = 64
QK_ROPE_DIM = 32
HALF_ROPE = QK_ROPE_DIM // 2
QK_DIM = QK_NOPE_DIM + QK_ROPE_DIM
Q_LORA_RANK = 256
KV_LORA_RANK = 128
ROPE_THETA = 10000.0
B_HEADS = 8
B_HEAD_DIM = 64
B_WIDTH = B_HEADS * B_HEAD_DIM
CHUNK = 128
EPS = 1e-6

LANES = 128
SUBLANES = 8
HEAD_W = A_HEADS * LANES
V_ROWS = 80

OFF_CQ = 0
OFF_CKV = OFF_CQ + Q_LORA_RANK
OFF_KR = OFF_CKV + KV_LORA_RANK
OFF_ZA = OFF_KR + LANES
OFF_U = OFF_ZA + A_WIDTH
OFF_V = OFF_U + B_WIDTH
OFF_ZB = OFF_V + B_WIDTH
D_IN_EXT = OFF_ZB + B_WIDTH

TOK_TILE = 512
W_T_BLOCK = 256
VEC_QHG, VEC_KHG, VEC_INVF, VEC_VGG = 0, LANES, 2 * LANES, 3 * LANES
VEC_WIDTH = VEC_VGG + B_WIDTH
ROPE_ROWS = TOK_TILE * QK_ROPE_DIM // LANES
Q_TILE = 512
KV_TILE = 1024
VMEM_LIMIT = 63 * 1024 * 1024
MAX_SCORE_BOUND = 48.0

f32 = jnp.float32
bf16 = jnp.bfloat16


def _gelu(t):
    return jax.nn.gelu(t)


def _silu(t):
    half = 0.5 * t
    return half + half * jnp.tanh(half)


def _inv_norm(t, width):
    return lax.rsqrt(jnp.sum(t * t, axis=-1, keepdims=True) + width * EPS)


def _proj_body(x_ref, pos_ref, g_in_ref, w_in_ref, gq_ref, w_uq_ref, gkv_ref, w_ukv_ref,
               vec_ref, ind_ref, wpair_ref, bpair_ref, gob_ref, sgn_ref,
               q_out, k_out, v_out, ga_out, mb_out, w_in_scr, ob_scr, gate_scr):
    step = pl.program_id(0)
    n_tiles = pl.num_programs(0) - 1

    @pl.when(step == 0)
    def _():
        for lo in range(0, D_IN_EXT, W_T_BLOCK):
            w_in_scr[:, lo:lo + W_T_BLOCK] = w_in_ref[lo:lo + W_T_BLOCK, :].T
        ob_scr[...] = jnp.zeros_like(ob_scr)
        gate_scr[...] = jnp.zeros_like(gate_scr)

    def finish_previous_tile():
        ob = ob_scr[...]
        obn = ob * _inv_norm(ob, B_WIDTH) * (gob_ref[...] * math.sqrt(B_WIDTH))
        mb_out[...] = (obn * gate_scr[...]).astype(bf16)

    @pl.when(step == n_tiles)
    def _():
        finish_previous_tile()

    @pl.when(step < n_tiles)
    def _():
        finish_previous_tile()
        _proj_tile(x_ref, pos_ref, g_in_ref, gq_ref, w_uq_ref, gkv_ref, w_ukv_ref, vec_ref, ind_ref, wpair_ref,
                   bpair_ref, sgn_ref, q_out, k_out, v_out, ga_out, w_in_scr, ob_scr, gate_scr)


def _proj_tile(x_ref, pos_ref, g_in_ref, gq_ref, w_uq_ref, gkv_ref, w_ukv_ref, vec_ref, ind_ref, wpair_ref,
               bpair_ref, sgn_ref, q_out, k_out, v_out, ga_out, w_in_scr, ob_scr, gate_scr):
    x = x_ref[...]
    h = (x * _inv_norm(x, D_MODEL) * (g_in_ref[...] * math.sqrt(D_MODEL))).astype(bf16)

    def proj(lo, width):
        return jnp.dot(h, w_in_scr[:, lo:lo + width], preferred_element_type=f32)

    c_q = proj(OFF_CQ, Q_LORA_RANK)
    c_kv = proj(OFF_CKV, KV_LORA_RANK)
    kr = proj(OFF_KR, LANES)

    lane = lax.broadcasted_iota(jnp.int32, (1, LANES), 1)
    rope_lanes = (lane >= QK_NOPE_DIM) & (lane < QK_DIM)
    ang = pos_ref[...].astype(f32) * vec_ref[:, VEC_INVF:VEC_INVF + LANES]
    cos_c, sin_c = jnp.cos(ang), jnp.sin(ang)
    cos_rows, sin_rows = [], []
    for g in range(LANES // QK_ROPE_DIM):
        shift = (QK_NOPE_DIM - QK_ROPE_DIM * g) % LANES
        cos_rows.append(pltpu.roll(cos_c, shift, 1) if shift else cos_c)
        sin_rows.append(pltpu.roll(sin_c, shift, 1) if shift else sin_c)
    cos = jnp.where(rope_lanes, jnp.concatenate(cos_rows, axis=0), 1.0)
    first = (lane >= QK_NOPE_DIM) & (lane < QK_NOPE_DIM + HALF_ROPE)
    sin_signed = jnp.concatenate(sin_rows, axis=0) * sgn_ref[...]

    def rope(t):
        swapped = jnp.where(first, pltpu.roll(t, LANES - HALF_ROPE, 1), pltpu.roll(t, HALF_ROPE, 1))
        return t * cos + swapped * sin_signed

    cqn = (c_q * _inv_norm(c_q, Q_LORA_RANK) * (gq_ref[...] * math.sqrt(Q_LORA_RANK))).astype(bf16)
    qhg = vec_ref[:, VEC_QHG:VEC_QHG + LANES]
    for pair in range(A_HEADS // 2):
        cols = slice(2 * pair * LANES, 2 * (pair + 1) * LANES)
        qq = jnp.dot(cqn, w_uq_ref[:, cols], preferred_element_type=f32)
        swap_cols = slice(HEAD_W + 2 * pair * LANES, HEAD_W + 2 * (pair + 1) * LANES)
        qs = jnp.dot(cqn, w_uq_ref[:, swap_cols], preferred_element_type=f32)
        for side in range(2):
            lanes = slice(side * LANES, (side + 1) * LANES)
            qh = qq[:, lanes] * cos + qs[:, lanes] * sin_signed
            q_out[0, 2 * pair + side] = (qh * _inv_norm(qh, QK_DIM) * qhg).astype(bf16)

    ckvn = (c_kv * _inv_norm(c_kv, KV_LORA_RANK) * (gkv_ref[...] * math.sqrt(KV_LORA_RANK))).astype(bf16)
    kpe = jnp.where(rope_lanes, rope(kr), 0.0)
    khg = vec_ref[:, VEC_KHG:VEC_KHG + LANES]
    nope_lanes = lane < QK_NOPE_DIM
    tail_row = lax.broadcasted_iota(jnp.int32, (V_ROWS - A_V_DIM, TOK_TILE), 0)
    v_tail = jnp.where(tail_row == 0, 1.0, 0.0).astype(bf16)
    for pair in range(A_HEADS // 2):
        kv = jnp.dot(ckvn, w_ukv_ref[:, 2 * pair * LANES:2 * (pair + 1) * LANES], preferred_element_type=f32)
        for side in range(2):
            hd = 2 * pair + side
            group = kv[:, side * LANES:(side + 1) * LANES]
            kh = jnp.where(nope_lanes, group, kpe)
            k_out[0, hd] = (kh * _inv_norm(kh, QK_DIM) * khg).astype(bf16)
            v_out[0, hd, 0:A_V_DIM, :] = group.T[QK_NOPE_DIM:].astype(bf16)
            v_out[0, hd, A_V_DIM:, :] = v_tail

    u = _gelu(proj(OFF_U, B_WIDTH))
    v = _gelu(proj(OFF_V, B_WIDTH))
    gate_b = _silu(proj(OFF_ZB, B_WIDTH))
    ssq = jnp.dot((v * v).astype(bf16), ind_ref[...], preferred_element_type=f32)
    ga_out[...] = _silu(proj(OFF_ZA, A_WIDTH)).astype(bf16)
    vn = (v * lax.rsqrt(ssq + B_HEAD_DIM * EPS) * (vec_ref[:, VEC_VGG:VEC_VGG + B_WIDTH] * math.sqrt(B_HEAD_DIM))).astype(bf16)
    lower = lane < B_HEAD_DIM
    zero = jnp.zeros((), bf16)
    chunks = []
    for c in range(0, TOK_TILE // CHUNK, 2):
        pairs = [[], []]
        for p in range(B_HEADS // 2):
            lanes = slice(p * LANES, (p + 1) * LANES)
            rhs = []
            for cc in (c, c + 1):
                vp = vn[cc * CHUNK:(cc + 1) * CHUNK, lanes]
                rhs.append(jnp.concatenate([jnp.where(lower, vp, zero), jnp.where(lower, zero, vp)], axis=0))
            sv = jnp.dot(wpair_ref[p], jnp.concatenate(rhs, axis=1), preferred_element_type=f32)
            for k, cc in enumerate((c, c + 1)):
                pairs[k].append(u[cc * CHUNK:(cc + 1) * CHUNK, lanes] * (sv[:, k * LANES:(k + 1) * LANES] + bpair_ref[p]))
        chunks += [jnp.concatenate(pairs[0], axis=1), jnp.concatenate(pairs[1], axis=1)]
    ob_scr[...] = jnp.concatenate(chunks, axis=0)
    gate_scr[...] = gate_b


def _group_b_projection(mb_ref, x_ref, w_out_ref):
    half = D_MODEL // 2
    return [x_ref[:, c:c + half] + jnp.dot(mb_ref[...], w_out_ref[A_WIDTH:, c:c + half], preferred_element_type=f32)
            for c in (0, half)]


def _attn_epilogue(acc_scr, y_b, ga_ref, goa_ref, w_out_ref, y_ref):
    half = D_MODEL // 2
    heads = []
    for hd in range(A_HEADS):
        acc = acc_scr[hd]
        heads.append(acc[0:A_V_DIM] / acc[A_V_DIM:A_V_DIM + 1])
    o = jnp.concatenate(heads, axis=0).T
    on = o * _inv_norm(o, A_WIDTH) * (goa_ref[...] * math.sqrt(A_WIDTH))
    mix_a = (on * ga_ref[...].astype(f32)).astype(bf16)
    for i, c in enumerate((0, half)):
        y_ref[:, c:c + half] = y_b[i] + jnp.dot(mix_a, w_out_ref[0:A_WIDTH, c:c + half], preferred_element_type=f32)


def _attn_bounded_body(bound_ref, q_ref, k_ref, v_ref, ga_ref, mb_ref, x_ref, goa_ref, w_out_ref, y_ref, acc_scr):
    seq = k_ref.shape[2]
    n_kv = seq // KV_TILE
    bound = jnp.tile(bound_ref[...], (1, q_ref.shape[2] // LANES))

    def one_head(hd):
        q_t = q_ref[0, hd].T
        acc = [None, None]
        s_next = jnp.dot(k_ref[0, hd, 0:KV_TILE, :], q_t, preferred_element_type=f32)
        for t in range(n_kv):
            s = s_next
            if t + 1 < n_kv:
                s_next = jnp.dot(k_ref[0, hd, (t + 1) * KV_TILE:(t + 2) * KV_TILE, :], q_t, preferred_element_type=f32)
            p = jnp.exp2(s - bound).astype(bf16)
            pv = jnp.dot(v_ref[0, hd, :, t * KV_TILE:(t + 1) * KV_TILE], p, preferred_element_type=f32)
            acc[t % 2] = pv if acc[t % 2] is None else acc[t % 2] + pv
        acc_scr[hd] = acc[0] + acc[1]

    for hd in range(A_HEADS):
        one_head(hd)
    _attn_epilogue(acc_scr, _group_b_projection(mb_ref, x_ref, w_out_ref), ga_ref, goa_ref, w_out_ref, y_ref)


def _attn_general_body(q_ref, k_ref, v_ref, ga_ref, mb_ref, x_ref, goa_ref, w_out_ref, y_ref,
                       s_even, s_odd, acc_scr):
    seq = k_ref.shape[2]
    n_kv = seq // KV_TILE

    def head_step(hd, parity, m_prev, do_scores=True, do_weighted=True):
        s_cur, s_prev = (s_even, s_odd) if parity == 0 else (s_odd, s_even)
        q_t = q_ref[0, hd].T if do_scores else None
        m = None
        acc = [None, None]
        for t in range(n_kv):
            keys = slice(t * KV_TILE, (t + 1) * KV_TILE)
            if do_scores:
                s = jnp.dot(k_ref[0, hd, keys, :], q_t, preferred_element_type=f32)
                s_cur[keys, :] = s
                s_max = jnp.max(s.reshape(KV_TILE // SUBLANES, SUBLANES, s.shape[1]), axis=0)
                m = s_max if m is None else jnp.maximum(m, s_max)
            if do_weighted:
                p = jnp.exp2(s_prev[keys, :] - m_prev).astype(bf16)
                pv = jnp.dot(v_ref[0, hd - 1, :, keys], p, preferred_element_type=f32)
                acc[t % 2] = pv if acc[t % 2] is None else acc[t % 2] + pv
        if do_weighted:
            acc_scr[hd - 1] = acc[0] + acc[1]
        return jnp.max(m, axis=0, keepdims=True) if do_scores else None

    def pair_step(i, m_prev):
        m_odd = head_step(2 * i + 1, 1, m_prev)
        return head_step(2 * i + 2, 0, m_odd)

    m_cur = head_step(0, 0, None, do_weighted=False)
    m_cur = lax.fori_loop(0, (A_HEADS - 2) // 2, pair_step, m_cur)
    m_cur = head_step(A_HEADS - 1, 1, m_cur)
    y_b = _group_b_projection(mb_ref, x_ref, w_out_ref)
    head_step(A_HEADS, 0, m_cur, do_scores=False)
    _attn_epilogue(acc_scr, y_b, ga_ref, goa_ref, w_out_ref, y_ref)


def _const_spec(shape):
    return pl.BlockSpec(shape, lambda *_: (0,) * len(shape))


def kernel(x, positions, norm_in_g, w_in, q_lora_g, w_uq, kv_lora_g, w_ukv, q_head_g, k_head_g,
           v_gate_g, w_s, b_s, out_a_g, out_b_g, w_out):
    batch, seq, _ = x.shape
    n_tok = batch * seq
    assert seq % Q_TILE == 0 and seq % KV_TILE == 0 and seq % TOK_TILE == 0 and TOK_TILE % CHUNK == 0

    w_t = w_in.T
    w_in_ext = jnp.concatenate(
        [w_t[:OFF_KR], jnp.zeros((QK_NOPE_DIM, D_MODEL), f32), w_t[OFF_KR:OFF_KR + QK_ROPE_DIM],
         jnp.zeros((LANES - QK_DIM, D_MODEL), f32), w_t[OFF_KR + QK_ROPE_DIM:]], axis=0).astype(bf16)

    uq = w_uq.reshape(Q_LORA_RANK, A_HEADS, QK_DIM)
    pad = jnp.zeros((Q_LORA_RANK, A_HEADS, LANES - QK_DIM), f32)
    uq_main = jnp.concatenate([uq, pad], axis=-1)
    uq_swap = jnp.concatenate([jnp.zeros_like(uq[..., :QK_NOPE_DIM]), uq[..., QK_NOPE_DIM + HALF_ROPE:],
                               uq[..., QK_NOPE_DIM:QK_NOPE_DIM + HALF_ROPE], pad], axis=-1)
    w_uq_ext = jnp.concatenate([uq_main, uq_swap], axis=1).reshape(Q_LORA_RANK, 2 * HEAD_W).astype(bf16)

    assert QK_NOPE_DIM + A_V_DIM == LANES

    qhg = jnp.pad(q_head_g * math.log2(math.e), (0, LANES - QK_DIM))
    khg = jnp.pad(k_head_g * math.sqrt(QK_DIM), (0, LANES - QK_DIM))

    inv_freq = 1.0 / (ROPE_THETA ** (jnp.arange(0, QK_ROPE_DIM, 2, dtype=f32) / QK_ROPE_DIM))
    invf = jnp.tile(inv_freq, LANES // HALF_ROPE)
    vec = jnp.concatenate([qhg, khg, invf, v_gate_g.reshape(B_WIDTH)]).reshape(1, VEC_WIDTH)
    sgn = np.zeros((1, LANES), np.float32)
    sgn[0, QK_NOPE_DIM:QK_NOPE_DIM + HALF_ROPE] = -1.0
    sgn[0, QK_NOPE_DIM + HALF_ROPE:QK_DIM] = 1.0
    sgn = jnp.asarray(sgn)

    head_of = np.arange(B_WIDTH) // B_HEAD_DIM
    ind = jnp.asarray((head_of[:, None] == head_of[None, :]).astype(np.float32)).astype(bf16)
    wpair = jnp.concatenate([w_s[0::2], w_s[1::2]], axis=2).astype(bf16)
    bpair = jnp.concatenate([jnp.broadcast_to(b_s[0::2, :, None], (B_HEADS // 2, CHUNK, B_HEAD_DIM)),
                             jnp.broadcast_to(b_s[1::2, :, None], (B_HEADS // 2, CHUNK, B_HEAD_DIM))], axis=2)

    x2 = x.reshape(n_tok, D_MODEL)
    groups = LANES // QK_ROPE_DIM
    pos2 = positions.reshape(n_tok // TOK_TILE, groups, ROPE_ROWS).transpose(0, 2, 1)
    pos2 = jnp.repeat(pos2, QK_ROPE_DIM, axis=2).reshape(n_tok // groups, LANES)
    tiles_per_row = seq // TOK_TILE

    n_tiles = n_tok // TOK_TILE
    tile = lambda i: jnp.minimum(i, n_tiles - 1)
    head_out = lambda t: pl.BlockSpec((1, A_HEADS, t, LANES),
                                      lambda i: (tile(i) // tiles_per_row, 0, tile(i) % tiles_per_row, 0))
    q, k, v, ga, mb = pl.pallas_call(
        _proj_body,
        grid=(n_tiles + 1,),
        in_specs=[
            pl.BlockSpec((TOK_TILE, D_MODEL), lambda i: (tile(i), 0)),
            pl.BlockSpec((ROPE_ROWS, LANES), lambda i: (tile(i), 0)),
            _const_spec((1, D_MODEL)),
            _const_spec((D_IN_EXT, D_MODEL)),
            _const_spec((1, Q_LORA_RANK)),
            _const_spec((Q_LORA_RANK, 2 * HEAD_W)),
            _const_spec((1, KV_LORA_RANK)),
            _const_spec((KV_LORA_RANK, HEAD_W)),
            _const_spec((1, VEC_WIDTH)),
            _const_spec((B_WIDTH, B_WIDTH)),
            _const_spec((B_HEADS // 2, CHUNK, 2 * CHUNK)),
            _const_spec((B_HEADS // 2, CHUNK, LANES)),
            _const_spec((1, B_WIDTH)),
            _const_spec((1, LANES)),
        ],
        out_specs=[
            head_out(TOK_TILE), head_out(TOK_TILE),
            pl.BlockSpec((1, A_HEADS, V_ROWS, TOK_TILE),
                         lambda i: (tile(i) // tiles_per_row, 0, 0, tile(i) % tiles_per_row)),
            pl.BlockSpec((TOK_TILE, A_WIDTH), lambda i: (tile(i), 0)),
            pl.BlockSpec((TOK_TILE, B_WIDTH), lambda i: (jnp.maximum(i - 1, 0), 0)),
        ],
        out_shape=[
            jax.ShapeDtypeStruct((batch, A_HEADS, seq, LANES), bf16),
            jax.ShapeDtypeStruct((batch, A_HEADS, seq, LANES), bf16),
            jax.ShapeDtypeStruct((batch, A_HEADS, V_ROWS, seq), bf16),
            jax.ShapeDtypeStruct((n_tok, A_WIDTH), bf16),
            jax.ShapeDtypeStruct((n_tok, B_WIDTH), bf16),
        ],
        scratch_shapes=[pltpu.VMEM((D_MODEL, D_IN_EXT), bf16), pltpu.VMEM((TOK_TILE, B_WIDTH), f32),
                        pltpu.VMEM((TOK_TILE, B_WIDTH), f32)],
        compiler_params=pltpu.CompilerParams(dimension_semantics=("arbitrary",), vmem_limit_bytes=VMEM_LIMIT),
        name="proj",
    )(x2, pos2, norm_in_g.reshape(1, D_MODEL), w_in_ext, q_lora_g.reshape(1, -1), w_uq_ext,
      kv_lora_g.reshape(1, -1), w_ukv.astype(bf16), vec, ind, wpair, bpair, out_b_g.reshape(1, B_WIDTH), sgn)

    q_tiles = seq // Q_TILE
    tok_blk = lambda w: pl.BlockSpec((Q_TILE, w), lambda b, i: (b * q_tiles + i, 0))
    resident = pl.Buffered(1)
    attn_specs = [
        pl.BlockSpec((1, A_HEADS, Q_TILE, LANES), lambda b, i: (b, 0, i, 0)),
        pl.BlockSpec((1, A_HEADS, seq, LANES), lambda b, i: (b, 0, 0, 0)),
        pl.BlockSpec((1, A_HEADS, V_ROWS, seq), lambda b, i: (b, 0, 0, 0)),
        tok_blk(A_WIDTH), tok_blk(B_WIDTH), tok_blk(D_MODEL),
        pl.BlockSpec((1, A_WIDTH), lambda b, i: (0, 0), pipeline_mode=resident),
        pl.BlockSpec((D_MODEL, D_MODEL), lambda b, i: (0, 0), pipeline_mode=resident),
    ]
    attn_args = (q, k, v, ga, mb, x2, out_a_g.reshape(1, A_WIDTH), w_out.astype(bf16))
    acc_scratch = pltpu.VMEM((A_HEADS, V_ROWS, Q_TILE), f32)

    def attention(body, extra_specs, scratch, name):
        return pl.pallas_call(
            body,
            grid=(batch, q_tiles),
            in_specs=extra_specs + attn_specs,
            out_specs=tok_blk(D_MODEL),
            out_shape=jax.ShapeDtypeStruct((n_tok, D_MODEL), f32),
            scratch_shapes=scratch,
            compiler_params=pltpu.CompilerParams(dimension_semantics=("arbitrary", "arbitrary"),
                                                 vmem_limit_bytes=VMEM_LIMIT),
            name=name,
        )

    score_bound = (jnp.max(jnp.abs(q_head_g)) * math.log2(math.e)) * (jnp.max(jnp.abs(k_head_g)) * math.sqrt(QK_DIM))
    score_bound = score_bound * (1.0 + 2.0 ** -6)
    y = lax.cond(
        score_bound <= MAX_SCORE_BOUND,
        lambda bound, *args: attention(_attn_bounded_body, [pl.BlockSpec((1, LANES), lambda b, i: (0, 0))],
                                       [acc_scratch], "attn_bounded")(jnp.full((1, LANES), bound, f32), *args),
        lambda bound, *args: attention(_attn_general_body, [],
                                       [pltpu.VMEM((seq, Q_TILE), f32), pltpu.VMEM((seq, Q_TILE), f32), acc_scratch],
                                       "attn")(*args),
        score_bound, *attn_args)
    return y.reshape(batch, seq, D_MODEL)
```

```python
import math

import jax
import jax.numpy as jnp
import numpy as np
from jax import lax
from jax.experimental import pallas as pl
from jax.experimental.pallas import tpu as pltpu

D_MODEL = 1024
A_HEADS = 8
A_V_DIM = 64
A_WIDTH = A_HEADS * A_V_DIM
QK_NOPE_DIM = 64
QK_ROPE_DIM = 32
HALF_ROPE = QK_ROPE_DIM // 2
QK_DIM = QK_NOPE_DIM + QK_ROPE_DIM
Q_LORA_RANK = 256
KV_LORA_RANK = 128
ROPE_THETA = 10000.0
B_HEADS = 8
B_HEAD_DIM = 64
B_WIDTH = B_HEADS * B_HEAD_DIM
CHUNK = 128
EPS = 1e-6

LANES = 128
SUBLANES = 8
HEAD_W = A_HEADS * LANES
V_ROWS = 80

OFF_CQ = 0
OFF_CKV = OFF_CQ + Q_LORA_RANK
OFF_KR = OFF_CKV + KV_LORA_RANK
OFF_ZA = OFF_KR + LANES
OFF_U = OFF_ZA + A_WIDTH
OFF_V = OFF_U + B_WIDTH
OFF_ZB = OFF_V + B_WIDTH
D_IN_EXT = OFF_ZB + B_WIDTH

TOK_TILE = 512
W_T_BLOCK = 256
VEC_QHG, VEC_KHG, VEC_INVF, VEC_VGG = 0, LANES, 2 * LANES, 3 * LANES
VEC_WIDTH = VEC_VGG + B_WIDTH
ROPE_ROWS = TOK_TILE * QK_ROPE_DIM // LANES
Q_TILE = 512
KV_TILE = 1024
VMEM_LIMIT = 63 * 1024 * 1024
MAX_SCORE_BOUND = 48.0

f32 = jnp.float32
bf16 = jnp.bfloat16


def _gelu(t):
    return jax.nn.gelu(t)


def _silu(t):
    half = 0.5 * t
    return half + half * jnp.tanh(half)


def _inv_norm(t, width):
    return lax.rsqrt(jnp.sum(t * t, axis=-1, keepdims=True) + width * EPS)


def _proj_body(x_ref, pos_ref, g_in_ref, w_in_ref, gq_ref, w_uq_ref, gkv_ref, w_ukv_ref,
               vec_ref, ind_ref, wpair_ref, bpair_ref, gob_ref, sgn_ref,
               q_out, k_out, v_out, ga_out, mb_out, w_in_scr):
    @pl.when(pl.program_id(0) == 0)
    def _():
        for lo in range(0, D_IN_EXT, W_T_BLOCK):
            w_in_scr[:, lo:lo + W_T_BLOCK] = w_in_ref[lo:lo + W_T_BLOCK, :].T

    x = x_ref[...]
    h = (x * _inv_norm(x, D_MODEL) * (g_in_ref[...] * math.sqrt(D_MODEL))).astype(bf16)

    def proj(lo, width):
        return jnp.dot(h, w_in_scr[:, lo:lo + width], preferred_element_type=f32)

    c_q = proj(OFF_CQ, Q_LORA_RANK)
    c_kv = proj(OFF_CKV, KV_LORA_RANK)
    kr = proj(OFF_KR, LANES)

    lane = lax.broadcasted_iota(jnp.int32, (1, LANES), 1)
    rope_lanes = (lane >= QK_NOPE_DIM) & (lane < QK_DIM)
    ang = pos_ref[...].astype(f32) * vec_ref[:, VEC_INVF:VEC_INVF + LANES]
    cos_c, sin_c = jnp.cos(ang), jnp.sin(ang)
    cos_rows, sin_rows = [], []
    for g in range(LANES // QK_ROPE_DIM):
        shift = (QK_NOPE_DIM - QK_ROPE_DIM * g) % LANES
        cos_rows.append(pltpu.roll(cos_c, shift, 1) if shift else cos_c)
        sin_rows.append(pltpu.roll(sin_c, shift, 1) if shift else sin_c)
    cos = jnp.where(rope_lanes, jnp.concatenate(cos_rows, axis=0), 1.0)
    first = (lane >= QK_NOPE_DIM) & (lane < QK_NOPE_DIM + HALF_ROPE)
    sin_signed = jnp.concatenate(sin_rows, axis=0) * sgn_ref[...]

    def rope(t):
        swapped = jnp.where(first, pltpu.roll(t, LANES - HALF_ROPE, 1), pltpu.roll(t, HALF_ROPE, 1))
        return t * cos + swapped * sin_signed

    cqn = (c_q * _inv_norm(c_q, Q_LORA_RANK) * (gq_ref[...] * math.sqrt(Q_LORA_RANK))).astype(bf16)
    qhg = vec_ref[:, VEC_QHG:VEC_QHG + LANES]
    for pair in range(A_HEADS // 2):
        cols = slice(2 * pair * LANES, 2 * (pair + 1) * LANES)
        qq = jnp.dot(cqn, w_uq_ref[:, cols], preferred_element_type=f32)
        swap_cols = slice(HEAD_W + 2 * pair * LANES, HEAD_W + 2 * (pair + 1) * LANES)
        qs = jnp.dot(cqn, w_uq_ref[:, swap_cols], preferred_element_type=f32)
        for side in range(2):
            lanes = slice(side * LANES, (side + 1) * LANES)
            qh = qq[:, lanes] * cos + qs[:, lanes] * sin_signed
            q_out[0, 2 * pair + side] = (qh * _inv_norm(qh, QK_DIM) * qhg).astype(bf16)

    ckvn = (c_kv * _inv_norm(c_kv, KV_LORA_RANK) * (gkv_ref[...] * math.sqrt(KV_LORA_RANK))).astype(bf16)
    kpe = jnp.where(rope_lanes, rope(kr), 0.0)
    khg = vec_ref[:, VEC_KHG:VEC_KHG + LANES]
    nope_lanes = lane < QK_NOPE_DIM
    tail_row = lax.broadcasted_iota(jnp.int32, (V_ROWS - A_V_DIM, TOK_TILE), 0)
    v_tail = jnp.where(tail_row == 0, 1.0, 0.0).astype(bf16)
    for pair in range(A_HEADS // 2):
        kv = jnp.dot(ckvn, w_ukv_ref[:, 2 * pair * LANES:2 * (pair + 1) * LANES], preferred_element_type=f32)
        for side in range(2):
            hd = 2 * pair + side
            group = kv[:, side * LANES:(side + 1) * LANES]
            kh = jnp.where(nope_lanes, group, kpe)
            k_out[0, hd] = (kh * _inv_norm(kh, QK_DIM) * khg).astype(bf16)
            v_out[0, hd, 0:A_V_DIM, :] = group.T[QK_NOPE_DIM:].astype(bf16)
            v_out[0, hd, A_V_DIM:, :] = v_tail

    u = _gelu(proj(OFF_U, B_WIDTH))
    v = _gelu(proj(OFF_V, B_WIDTH))
    gate_b = _silu(proj(OFF_ZB, B_WIDTH))
    ssq = jnp.dot((v * v).astype(bf16), ind_ref[...], preferred_element_type=f32)
    ga_out[...] = _silu(proj(OFF_ZA, A_WIDTH)).astype(bf16)
    vn = (v * lax.rsqrt(ssq + B_HEAD_DIM * EPS) * (vec_ref[:, VEC_VGG:VEC_VGG + B_WIDTH] * math.sqrt(B_HEAD_DIM))).astype(bf16)
    lower = lane < B_HEAD_DIM
    zero = jnp.zeros((), bf16)
    chunks = []
    for c in range(0, TOK_TILE // CHUNK, 2):
        pairs = [[], []]
        for p in range(B_HEADS // 2):
            lanes = slice(p * LANES, (p + 1) * LANES)
            rhs = []
            for cc in (c, c + 1):
                vp = vn[cc * CHUNK:(cc + 1) * CHUNK, lanes]
                rhs.append(jnp.concatenate([jnp.where(lower, vp, zero), jnp.where(lower, zero, vp)], axis=0))
            sv = jnp.dot(wpair_ref[p], jnp.concatenate(rhs, axis=1), preferred_element_type=f32)
            for k, cc in enumerate((c, c + 1)):
                pairs[k].append(u[cc * CHUNK:(cc + 1) * CHUNK, lanes] * (sv[:, k * LANES:(k + 1) * LANES] + bpair_ref[p]))
        chunks += [jnp.concatenate(pairs[0], axis=1), jnp.concatenate(pairs[1], axis=1)]
    ob = jnp.concatenate(chunks, axis=0)
    obn = ob * _inv_norm(ob, B_WIDTH) * (gob_ref[...] * math.sqrt(B_WIDTH))
    mb_out[...] = (obn * gate_b).astype(bf16)


def _group_b_projection(mb_ref, x_ref, w_out_ref):
    half = D_MODEL // 2
    return [x_ref[:, c:c + half] + jnp.dot(mb_ref[...], w_out_ref[A_WIDTH:, c:c + half], preferred_element_type=f32)
            for c in (0, half)]


def _attn_epilogue(acc_scr, y_b, ga_ref, goa_ref, w_out_ref, y_ref):
    half = D_MODEL // 2
    heads = []
    for hd in range(A_HEADS):
        acc = acc_scr[hd]
        heads.append(acc[0:A_V_DIM] / acc[A_V_DIM:A_V_DIM + 1])
    o = jnp.concatenate(heads, axis=0).T
    on = o * _inv_norm(o, A_WIDTH) * (goa_ref[...] * math.sqrt(A_WIDTH))
    mix_a = (on * ga_ref[...].astype(f32)).astype(bf16)
    for i, c in enumerate((0, half)):
        y_ref[:, c:c + half] = y_b[i] + jnp.dot(mix_a, w_out_ref[0:A_WIDTH, c:c + half], preferred_element_type=f32)


def _attn_bounded_body(bound_ref, q_ref, k_ref, v_ref, ga_ref, mb_ref, x_ref, goa_ref, w_out_ref, y_ref, acc_scr):
    seq = k_ref.shape[2]
    n_kv = seq // KV_TILE
    bound = jnp.tile(bound_ref[...], (1, q_ref.shape[2] // LANES))

    def one_head(hd):
        q_t = q_ref[0, hd].T
        probs = []
        for t in range(n_kv):
            s = jnp.dot(k_ref[0, hd, t * KV_TILE:(t + 1) * KV_TILE, :], q_t, preferred_element_type=f32)
            probs.append(jnp.exp2(s - bound).astype(bf16))
        acc_scr[hd] = jnp.dot(v_ref[0, hd], jnp.concatenate(probs, axis=0), preferred_element_type=f32)

    for hd in range(A_HEADS):
        one_head(hd)
    _attn_epilogue(acc_scr, _group_b_projection(mb_ref, x_ref, w_out_ref), ga_ref, goa_ref, w_out_ref, y_ref)


def _attn_general_body(q_ref, k_ref, v_ref, ga_ref, mb_ref, x_ref, goa_ref, w_out_ref, y_ref,
                       s_even, s_odd, acc_scr):
    seq = k_ref.shape[2]
    n_kv = seq // KV_TILE

    def head_step(hd, parity, m_prev, do_scores=True, do_weighted=True):
        s_cur, s_prev = (s_even, s_odd) if parity == 0 else (s_odd, s_even)
        q_t = q_ref[0, hd].T if do_scores else None
        m = None
        acc = [None, None]
        for t in range(n_kv):
            keys = slice(t * KV_TILE, (t + 1) * KV_TILE)
            if do_scores:
                s = jnp.dot(k_ref[0, hd, keys, :], q_t, preferred_element_type=f32)
                s_cur[keys, :] = s
                s_max = jnp.max(s.reshape(KV_TILE // SUBLANES, SUBLANES, s.shape[1]), axis=0)
                m = s_max if m is None else jnp.maximum(m, s_max)
            if do_weighted:
                p = jnp.exp2(s_prev[keys, :] - m_prev).astype(bf16)
                pv = jnp.dot(v_ref[0, hd - 1, :, keys], p, preferred_element_type=f32)
                acc[t % 2] = pv if acc[t % 2] is None else acc[t % 2] + pv
        if do_weighted:
            acc_scr[hd - 1] = acc[0] + acc[1]
        return jnp.max(m, axis=0, keepdims=True) if do_scores else None

    def pair_step(i, m_prev):
        m_odd = head_step(2 * i + 1, 1, m_prev)
        return head_step(2 * i + 2, 0, m_odd)

    m_cur = head_step(0, 0, None, do_weighted=False)
    m_cur = lax.fori_loop(0, (A_HEADS - 2) // 2, pair_step, m_cur)
    m_cur = head_step(A_HEADS - 1, 1, m_cur)
    y_b = _group_b_projection(mb_ref, x_ref, w_out_ref)
    head_step(A_HEADS, 0, m_cur, do_scores=False)
    _attn_epilogue(acc_scr, y_b, ga_ref, goa_ref, w_out_ref, y_ref)


def _const_spec(shape):
    return pl.BlockSpec(shape, lambda *_: (0,) * len(shape))


def kernel(x, positions, norm_in_g, w_in, q_lora_g, w_uq, kv_lora_g, w_ukv, q_head_g, k_head_g,
           v_gate_g, w_s, b_s, out_a_g, out_b_g, w_out):
    batch, seq, _ = x.shape
    n_tok = batch * seq
    assert seq % Q_TILE == 0 and seq % KV_TILE == 0 and seq % TOK_TILE == 0 and TOK_TILE % CHUNK == 0

    w_t = w_in.T
    w_in_ext = jnp.concatenate(
        [w_t[:OFF_KR], jnp.zeros((QK_NOPE_DIM, D_MODEL), f32), w_t[OFF_KR:OFF_KR + QK_ROPE_DIM],
         jnp.zeros((LANES - QK_DIM, D_MODEL), f32), w_t[OFF_KR + QK_ROPE_DIM:]], axis=0).astype(bf16)

    uq = w_uq.reshape(Q_LORA_RANK, A_HEADS, QK_DIM)
    pad = jnp.zeros((Q_LORA_RANK, A_HEADS, LANES - QK_DIM), f32)
    uq_main = jnp.concatenate([uq, pad], axis=-1)
    uq_swap = jnp.concatenate([jnp.zeros_like(uq[..., :QK_NOPE_DIM]), uq[..., QK_NOPE_DIM + HALF_ROPE:],
                               uq[..., QK_NOPE_DIM:QK_NOPE_DIM + HALF_ROPE], pad], axis=-1)
    w_uq_ext = jnp.concatenate([uq_main, uq_swap], axis=1).reshape(Q_LORA_RANK, 2 * HEAD_W).astype(bf16)

    assert QK_NOPE_DIM + A_V_DIM == LANES

    qhg = jnp.pad(q_head_g * math.log2(math.e), (0, LANES - QK_DIM))
    khg = jnp.pad(k_head_g * math.sqrt(QK_DIM), (0, LANES - QK_DIM))

    inv_freq = 1.0 / (ROPE_THETA ** (jnp.arange(0, QK_ROPE_DIM, 2, dtype=f32) / QK_ROPE_DIM))
    invf = jnp.tile(inv_freq, LANES // HALF_ROPE)
    vec = jnp.concatenate([qhg, khg, invf, v_gate_g.reshape(B_WIDTH)]).reshape(1, VEC_WIDTH)
    sgn = np.zeros((1, LANES), np.float32)
    sgn[0, QK_NOPE_DIM:QK_NOPE_DIM + HALF_ROPE] = -1.0
    sgn[0, QK_NOPE_DIM + HALF_ROPE:QK_DIM] = 1.0
    sgn = jnp.asarray(sgn)

    head_of = np.arange(B_WIDTH) // B_HEAD_DIM
    ind = jnp.asarray((head_of[:, None] == head_of[None, :]).astype(np.float32)).astype(bf16)
    wpair = jnp.concatenate([w_s[0::2], w_s[1::2]], axis=2).astype(bf16)
    bpair = jnp.concatenate([jnp.broadcast_to(b_s[0::2, :, None], (B_HEADS // 2, CHUNK, B_HEAD_DIM)),
                             jnp.broadcast_to(b_s[1::2, :, None], (B_HEADS // 2, CHUNK, B_HEAD_DIM))], axis=2)

    x2 = x.reshape(n_tok, D_MODEL)
    groups = LANES // QK_ROPE_DIM
    pos2 = positions.reshape(n_tok // TOK_TILE, groups, ROPE_ROWS).transpose(0, 2, 1)
    pos2 = jnp.repeat(pos2, QK_ROPE_DIM, axis=2).reshape(n_tok // groups, LANES)
    tiles_per_row = seq // TOK_TILE

    head_out = lambda t: pl.BlockSpec((1, A_HEADS, t, LANES), lambda i: (i // tiles_per_row, 0, i % tiles_per_row, 0))
    q, k, v, ga, mb = pl.pallas_call(
        _proj_body,
        grid=(n_tok // TOK_TILE,),
        in_specs=[
            pl.BlockSpec((TOK_TILE, D_MODEL), lambda i: (i, 0)),
            pl.BlockSpec((ROPE_ROWS, LANES), lambda i: (i, 0)),
            _const_spec((1, D_MODEL)),
            _const_spec((D_IN_EXT, D_MODEL)),
            _const_spec((1, Q_LORA_RANK)),
            _const_spec((Q_LORA_RANK, 2 * HEAD_W)),
            _const_spec((1, KV_LORA_RANK)),
            _const_spec((KV_LORA_RANK, HEAD_W)),
            _const_spec((1, VEC_WIDTH)),
            _const_spec((B_WIDTH, B_WIDTH)),
            _const_spec((B_HEADS // 2, CHUNK, 2 * CHUNK)),
            _const_spec((B_HEADS // 2, CHUNK, LANES)),
            _const_spec((1, B_WIDTH)),
            _const_spec((1, LANES)),
        ],
        out_specs=[
            head_out(TOK_TILE), head_out(TOK_TILE),
            pl.BlockSpec((1, A_HEADS, V_ROWS, TOK_TILE), lambda i: (i // tiles_per_row, 0, 0, i % tiles_per_row)),
            pl.BlockSpec((TOK_TILE, A_WIDTH), lambda i: (i, 0)),
            pl.BlockSpec((TOK_TILE, B_WIDTH), lambda i: (i, 0)),
        ],
        out_shape=[
            jax.ShapeDtypeStruct((batch, A_HEADS, seq, LANES), bf16),
            jax.ShapeDtypeStruct((batch, A_HEADS, seq, LANES), bf16),
            jax.ShapeDtypeStruct((batch, A_HEADS, V_ROWS, seq), bf16),
            jax.ShapeDtypeStruct((n_tok, A_WIDTH), bf16),
            jax.ShapeDtypeStruct((n_tok, B_WIDTH), bf16),
        ],
        scratch_shapes=[pltpu.VMEM((D_MODEL, D_IN_EXT), bf16)],
        compiler_params=pltpu.CompilerParams(dimension_semantics=("arbitrary",), vmem_limit_bytes=VMEM_LIMIT),
        name="proj",
    )(x2, pos2, norm_in_g.reshape(1, D_MODEL), w_in_ext, q_lora_g.reshape(1, -1), w_uq_ext,
      kv_lora_g.reshape(1, -1), w_ukv.astype(bf16), vec, ind, wpair, bpair, out_b_g.reshape(1, B_WIDTH), sgn)

    q_tiles = seq // Q_TILE
    tok_blk = lambda w: pl.BlockSpec((Q_TILE, w), lambda b, i: (b * q_tiles + i, 0))
    resident = pl.Buffered(1)
    attn_specs = [
        pl.BlockSpec((1, A_HEADS, Q_TILE, LANES), lambda b, i: (b, 0, i, 0)),
        pl.BlockSpec((1, A_HEADS, seq, LANES), lambda b, i: (b, 0, 0, 0)),
        pl.BlockSpec((1, A_HEADS, V_ROWS, seq), lambda b, i: (b, 0, 0, 0)),
        tok_blk(A_WIDTH), tok_blk(B_WIDTH), tok_blk(D_MODEL),
        pl.BlockSpec((1, A_WIDTH), lambda b, i: (0, 0), pipeline_mode=resident),
        pl.BlockSpec((D_MODEL, D_MODEL), lambda b, i: (0, 0), pipeline_mode=resident),
    ]
    attn_args = (q, k, v, ga, mb, x2, out_a_g.reshape(1, A_WIDTH), w_out.astype(bf16))
    acc_scratch = pltpu.VMEM((A_HEADS, V_ROWS, Q_TILE), f32)

    def attention(body, extra_specs, scratch, name):
        return pl.pallas_call(
            body,
            grid=(batch, q_tiles),
            in_specs=extra_specs + attn_specs,
            out_specs=tok_blk(D_MODEL),
            out_shape=jax.ShapeDtypeStruct((n_tok, D_MODEL), f32),
            scratch_shapes=scratch,
            compiler_params=pltpu.CompilerParams(dimension_semantics=("arbitrary", "arbitrary"),
                                                 vmem_limit_bytes=VMEM_LIMIT),
            name=name,
        )

    score_bound = (jnp.max(jnp.abs(q_head_g)) * math.log2(math.e)) * (jnp.max(jnp.abs(k_head_g)) * math.sqrt(QK_DIM))
    score_bound = score_bound * (1.0 + 2.0 ** -6)
    y = lax.cond(
        score_bound <= MAX_SCORE_BOUND,
        lambda bound, *args: attention(_attn_bounded_body, [pl.BlockSpec((1, LANES), lambda b, i: (0, 0))],
                                       [acc_scratch], "attn_bounded")(jnp.full((1, LANES), bound, f32), *args),
        lambda bound, *args: attention(_attn_general_body, [],
                                       [pltpu.VMEM((seq, Q_TILE), f32), pltpu.VMEM((seq, Q_TILE), f32), acc_scratch],
                                       "attn")(*args),
        score_bound, *attn_args)
    return y.reshape(batch, seq, D_MODEL)
```
